```python
import math
import jax, jax.numpy as jnp
from jax import lax
import numpy as np


D_MODEL = 2048
BATCH = 2
SEQ = 4096
DEPTH = 4
DEC_BATCH = 8
DEC_SEQ = 4
PAST_LEN = 16384
PAGE_SIZE = 128

N_A_LAYERS = DEPTH // 2
N_B_LAYERS = DEPTH - N_A_LAYERS
N_HEADS = 16
HEAD_DIM = D_MODEL // N_HEADS
CONV_W = 3
D_FF = ((8 * D_MODEL // 3 + 255) // 256) * 256
MOBA_BLOCK = 256
MOBA_TOPK = 3
Q_BLOCK = 16
ROPE_THETA = 10000.0
LN_EPS = 1e-5
ALPHA = (2 * DEPTH) ** 0.25
BETA = (8 * DEPTH) ** -0.25
NEG_INF = -1e30

kernel_name = 'yoco_shortconv_moba_macaron_deepnorm_step'


def layer_norm(z, g, b):
    zf = z.astype(jnp.float32)
    mu = jnp.mean(zf, axis=-1, keepdims=True)
    var = jnp.mean(jnp.square(zf - mu), axis=-1, keepdims=True)
    out = (zf - mu) * lax.rsqrt(var + LN_EPS) * g.astype(jnp.float32) + b.astype(jnp.float32)
    return out.astype(z.dtype)


def post_norm(x, h, g, b):
    return layer_norm(ALPHA * x + h, g, b)


def swiglu(x, w_gate, w_up, w_down):
    return (jax.nn.silu(x @ w_gate) * (x @ w_up)) @ w_down


def rope(x, pos):
    half = HEAD_DIM // 2
    inv = ROPE_THETA ** (-2.0 * jnp.arange(half, dtype=jnp.float32) / HEAD_DIM)
    ang = pos.astype(jnp.float32)[:, None] * inv[None, :]
    cos = jnp.cos(ang)[None, :, None, :]
    sin = jnp.sin(ang)[None, :, None, :]
    xf = x.astype(jnp.float32)
    x1, x2 = xf[..., :half], xf[..., half:]
    return jnp.concatenate([x1 * cos - x2 * sin, x2 * cos + x1 * sin], axis=-1).astype(x.dtype)


def short_conv(x, prev, w_in, w_c, w_out):
    t = x.shape[1]
    gate_b, gate_c, h = jnp.split(x @ w_in, 3, axis=-1)
    u = gate_c * h
    ue = jnp.concatenate([prev.astype(u.dtype), u], axis=1)
    y = w_c[0] * ue[:, 0:t]
    for j in range(1, CONV_W):
        y = y + w_c[j] * ue[:, j:j + t]
    return (gate_b * y) @ w_out, ue[:, t:]


def shared_kv(x, pos, past_k, past_v, w_k, w_v):
    bsz, t, _ = x.shape
    k = rope((x @ w_k).reshape(bsz, t, N_HEADS, HEAD_DIM), pos)
    v = (x @ w_v).reshape(bsz, t, N_HEADS, HEAD_DIM)
    p = 0 if past_k is None else past_k.shape[1]
    pad = (-(p + t)) % MOBA_BLOCK
    nb = (p + t + pad) // MOBA_BLOCK
    zeros = jnp.zeros((bsz, pad, N_HEADS, HEAD_DIM), k.dtype)
    k_parts = [k, zeros] if past_k is None else [past_k.astype(k.dtype), k, zeros]
    v_parts = [v, zeros] if past_v is None else [past_v.astype(v.dtype), v, zeros]
    kb = jnp.concatenate(k_parts, axis=1).reshape(bsz, nb, MOBA_BLOCK, N_HEADS, HEAD_DIM)
    vb = jnp.concatenate(v_parts, axis=1).reshape(bsz, nb, MOBA_BLOCK, N_HEADS, HEAD_DIM)
    means = jnp.mean(kb.astype(jnp.float32), axis=2)
    return k, v, kb, vb, means


def moba_attend(q, kb, vb, means, pos):
    bsz, nh, nq, _ = q.shape
    nb = kb.shape[1]
    own = pos // MOBA_BLOCK
    gate = jnp.einsum('bhqd,bnhd->bhqn', q.astype(jnp.float32), means)
    fully_past = jnp.arange(nb)[None, :] < own[:, None]
    gate = jnp.where(fully_past[None, None], gate, NEG_INF)
    k_sel = min(MOBA_TOPK, nb)
    _, idx = lax.top_k(gate, k_sel)
    slot_ok = jnp.arange(k_sel)[None, :] < own[:, None]
    own_idx = jnp.broadcast_to(own[None, None, :, None], (bsz, nh, nq, 1)).astype(idx.dtype)
    idx_all = jnp.concatenate([idx, own_idx], axis=-1)
    ok_all = jnp.concatenate([slot_ok, jnp.ones((nq, 1), bool)], axis=-1)
    bi = jnp.arange(bsz)[:, None, None, None]
    hi = jnp.arange(nh)[None, :, None, None]
    kg = kb[bi, idx_all, :, hi, :]
    vg = vb[bi, idx_all, :, hi, :]
    logits = jnp.einsum('bhqd,bhqskd->bhqsk', q, kg,
                        preferred_element_type=jnp.float32) * (HEAD_DIM ** -0.5)
    key_pos = idx_all[..., None] * MOBA_BLOCK + jnp.arange(MOBA_BLOCK)
    mask = ok_all[None, None, :, :, None] & (key_pos <= pos[None, None, :, None, None])
    logits = jnp.where(mask, logits, NEG_INF)
    n_slots = idx_all.shape[-1]
    probs = jax.nn.softmax(logits.reshape(bsz, nh, nq, n_slots * MOBA_BLOCK), axis=-1)
    probs = probs.reshape(bsz, nh, nq, n_slots, MOBA_BLOCK).astype(vg.dtype)
    return jnp.einsum('bhqsk,bhqskd->bhqd', probs, vg)


def moba_mixer(x, pos, kb, vb, means, w_q, w_o, qb):
    bsz, t, _ = x.shape
    q = rope((x @ w_q).reshape(bsz, t, N_HEADS, HEAD_DIM), pos).transpose(0, 2, 1, 3)
    nc = t // qb
    qc = q.reshape(bsz, N_HEADS, nc, qb, HEAD_DIM).transpose(2, 0, 1, 3, 4)
    pc = pos.reshape(nc, qb)
    out = lax.map(lambda a: moba_attend(a[0], kb, vb, means, a[1]), (qc, pc))
    out = out.transpose(1, 0, 3, 2, 4).reshape(bsz, t, N_HEADS * HEAD_DIM)
    return out @ w_o


def trunk(x, pos, conv_prev, past_k, past_v, ln_g, ln_b, w_ffn_gate, w_ffn_up, w_ffn_down,
          w_conv_in, w_conv, w_conv_out, w_k, w_v, w_q, w_o):
    t = x.shape[1]
    qb = math.gcd(t, Q_BLOCK)
    conv_states = []
    for layer in range(DEPTH):
        if layer == N_A_LAYERS:
            k_new, v_new, kb, vb, means = shared_kv(x, pos, past_k, past_v, w_k, w_v)
        x = post_norm(x, 0.5 * swiglu(x, w_ffn_gate[layer, 0], w_ffn_up[layer, 0], w_ffn_down[layer, 0]),
                      ln_g[layer, 0], ln_b[layer, 0])
        if layer < N_A_LAYERS:
            mix, st = short_conv(x, conv_prev[layer], w_conv_in[layer], w_conv[layer], w_conv_out[layer])
            conv_states.append(st)
        else:
            j = layer - N_A_LAYERS
            mix = moba_mixer(x, pos, kb, vb, means, w_q[j], w_o[j], qb)
        x = post_norm(x, mix, ln_g[layer, 1], ln_b[layer, 1])
        x = post_norm(x, 0.5 * swiglu(x, w_ffn_gate[layer, 1], w_ffn_up[layer, 1], w_ffn_down[layer, 1]),
                      ln_g[layer, 2], ln_b[layer, 2])
    return x, jnp.stack(conv_states), k_new, v_new


def setup_inputs(seed: int = 0) -> dict:
    key = jax.random.key(seed)
    ks = jax.random.split(key, 18)
    n_pages = PAST_LEN // PAGE_SIZE
    n_used = DEC_BATCH * n_pages
    n_phys = n_used + max(1, n_used // 4)
    hd = N_HEADS * HEAD_DIM

    def nrm(k, shape, s):
        return jax.random.normal(k, shape, jnp.float32) * s

    page_table = jax.random.permutation(ks[5], n_phys)[:n_used].reshape(DEC_BATCH, n_pages).astype(jnp.int32)
    return {
        'x_prompt': nrm(ks[0], (BATCH, SEQ, D_MODEL), 1.0),
        'x_sample': nrm(ks[1], (DEC_BATCH, DEC_SEQ, D_MODEL), 1.0),
        'state_conv': nrm(ks[2], (N_A_LAYERS, DEC_BATCH, CONV_W - 1, D_MODEL), 1.0),
        'cache_k': nrm(ks[3], (n_phys, PAGE_SIZE, N_HEADS, HEAD_DIM), 1.0),
        'cache_v': nrm(ks[4], (n_phys, PAGE_SIZE, N_HEADS, HEAD_DIM), BETA),
        'page_table': page_table,
        'ln_g': 1.0 + nrm(ks[6], (DEPTH, 3, D_MODEL), 0.02),
        'ln_b': nrm(ks[7], (DEPTH, 3, D_MODEL), 0.02),
        'w_ffn_gate': nrm(ks[8], (DEPTH, 2, D_MODEL, D_FF), D_MODEL ** -0.5),
        'w_ffn_up': nrm(ks[9], (DEPTH, 2, D_MODEL, D_FF), D_MODEL ** -0.5),
        'w_ffn_down': nrm(ks[10], (DEPTH, 2, D_FF, D_MODEL), BETA * D_FF ** -0.5),
        'w_conv_in': nrm(ks[11], (N_A_LAYERS, D_MODEL, 3 * D_MODEL), D_MODEL ** -0.5),
        'w_conv': nrm(ks[12], (N_A_LAYERS, CONV_W, D_MODEL), CONV_W ** -0.5),
        'w_conv_out': nrm(ks[13], (N_A_LAYERS, D_MODEL, D_MODEL), BETA * D_MODEL ** -0.5),
        'w_k': nrm(ks[14], (D_MODEL, hd), D_MODEL ** -0.5),
        'w_v': nrm(ks[15], (D_MODEL, hd), BETA * D_MODEL ** -0.5),
        'w_q': nrm(ks[16], (N_B_LAYERS, D_MODEL, hd), D_MODEL ** -0.5),
        'w_o': nrm(ks[17], (N_B_LAYERS, hd, D_MODEL), BETA * hd ** -0.5),
    }


def reference(x_prompt, x_sample, state_conv, cache_k, cache_v, page_table, ln_g, ln_b,
              w_ffn_gate, w_ffn_up, w_ffn_down, w_conv_in, w_conv, w_conv_out, w_k, w_v, w_q, w_o):
    bp, tp, _ = x_prompt.shape
    pos_p = jnp.arange(tp, dtype=jnp.int32)
    conv0 = jnp.zeros((N_A_LAYERS, bp, CONV_W - 1, D_MODEL), x_prompt.dtype)
    y_prompt, conv_prompt, k_prompt, v_prompt = trunk(
        x_prompt, pos_p, conv0, None, None, ln_g, ln_b, w_ffn_gate, w_ffn_up, w_ffn_down,
        w_conv_in, w_conv, w_conv_out, w_k, w_v, w_q, w_o)

    bs, ts, _ = x_sample.shape
    n_pages = page_table.shape[1]
    past_len = n_pages * PAGE_SIZE
    past_k = cache_k[page_table].reshape(bs, past_len, N_HEADS, HEAD_DIM)
    past_v = cache_v[page_table].reshape(bs, past_len, N_HEADS, HEAD_DIM)
    pos_s = past_len + jnp.arange(ts, dtype=jnp.int32)
    y_sample, conv_sample, k_sample, v_sample = trunk(
        x_sample, pos_s, state_conv, past_k, past_v, ln_g, ln_b, w_ffn_gate, w_ffn_up, w_ffn_down,
        w_conv_in, w_conv, w_conv_out, w_k, w_v, w_q, w_o)

    return (y_prompt, y_sample, conv_prompt, conv_sample, k_prompt, v_prompt, k_sample, v_sample)
```

```python
import functools

import jax
import jax.numpy as jnp
from jax import lax
from jax.experimental import pallas as pl
from jax.experimental.pallas import tpu as pltpu

F32 = jnp.float32
BF16 = jnp.bfloat16

MOBA_BLOCK = 256
MOBA_TOPK = 3
CONV_W = 3
ROPE_THETA = 10000.0
LN_EPS = 1e-5
NEG_INF = -1e30

SUBLANES = 8
LANES = 128
VMEM_LIMIT_BYTES = 60 * 1024 * 1024

_NT = (((1,), (1,)), ((), ()))


def _params(*sem):
    return pltpu.CompilerParams(dimension_semantics=sem, vmem_limit_bytes=VMEM_LIMIT_BYTES)


def _row_tile_spec(tm, d):
    return pl.BlockSpec((tm, d), lambda i, j: (i, 0), pipeline_mode=pl.Buffered(1))


def _layer_norm(z, g, b):
    mu = jnp.mean(z, axis=-1, keepdims=True)
    d = z - mu
    var = jnp.mean(d * d, axis=-1, keepdims=True)
    return d * lax.rsqrt(var + LN_EPS) * g + b


def _cast_rows(src_ref, dst_ref, rows):
    n = src_ref.shape[0] // rows

    def body(c, carry):
        r = pl.ds(pl.multiple_of(c * rows, rows), rows)
        dst_ref[r, :] = src_ref[r, :].astype(dst_ref.dtype)
        return carry

    lax.fori_loop(0, n, body, 0)


def _accumulate(o_ref, part, j):
    @pl.when(j == 0)
    def _():
        o_ref[...] = part

    @pl.when(j > 0)
    def _():
        o_ref[...] += part


def _post_norm_rows(x_ref, o_ref, g_ref, b_ref, alpha, scale, rows):
    n = o_ref.shape[0] // rows
    g = g_ref[...]
    b = b_ref[...]

    def body(c, carry):
        r = pl.ds(pl.multiple_of(c * rows, rows), rows)
        z = alpha * x_ref[r, :] + scale * o_ref[r, :]
        o_ref[r, :] = _layer_norm(z, g, b)
        return carry

    lax.fori_loop(0, n, body, 0)


def _row_chunk(tm):
    return min(tm, 64)


def _ffn_body(x_ref, wg_ref, wu_ref, wd_ref, g_ref, b_ref, o_ref, xb_ref, *, alpha, n_j):
    j = pl.program_id(1)

    @pl.when(j == 0)
    def _():
        _cast_rows(x_ref, xb_ref, _row_chunk(x_ref.shape[0]))

    xb = xb_ref[...]
    gate = jnp.dot(xb, wg_ref[...].astype(BF16), preferred_element_type=F32)
    up = jnp.dot(xb, wu_ref[...].astype(BF16), preferred_element_type=F32)
    h = (jax.nn.silu(gate) * up).astype(BF16)
    part = jnp.dot(h, wd_ref[...].astype(BF16), preferred_element_type=F32)
    _accumulate(o_ref, part, j)

    @pl.when(j == n_j - 1)
    def _():
        _post_norm_rows(x_ref, o_ref, g_ref, b_ref, alpha, 0.5, _row_chunk(x_ref.shape[0]))


def _ffn(x, w_gate, w_up, w_down, lng, lnb, layer, sub, ln_idx, *, alpha, tm, tf):
    m, d = x.shape
    f = w_gate.shape[-1]
    n_i, n_j = m // tm, f // tf
    return pl.pallas_call(
        functools.partial(_ffn_body, alpha=alpha, n_j=n_j),
        grid=(n_i, n_j),
        in_specs=[
            _row_tile_spec(tm, d),
            pl.BlockSpec((None, None, d, tf), lambda i, j: (layer, sub, 0, j)),
            pl.BlockSpec((None, None, d, tf), lambda i, j: (layer, sub, 0, j)),
            pl.BlockSpec((None, None, tf, d), lambda i, j: (layer, sub, j, 0)),
            pl.BlockSpec((None, 1, d), lambda i, j: (ln_idx, 0, 0)),
            pl.BlockSpec((None, 1, d), lambda i, j: (ln_idx, 0, 0)),
        ],
        out_specs=pl.BlockSpec((tm, d), lambda i, j: (i, 0)),
        out_shape=jax.ShapeDtypeStruct((m, d), F32),
        scratch_shapes=[pltpu.VMEM((tm, d), BF16)],
        compiler_params=_params("parallel", "arbitrary"),
        name="ffn",
    )(x, w_gate, w_up, w_down, lng, lnb)


def _conv_body(*refs, alpha, n_j, tiles_per_seq, seq_len, multi_seq):
    if multi_seq:
        (x_ref, wb_ref, wc_ref, wh_ref, wk_ref, wo_ref, g_ref, b_ref, p1_ref, p2_ref,
         o_ref, tail_ref, xb_ref, ue_ref, halo_ref) = refs
    else:
        (x_ref, wb_ref, wc_ref, wh_ref, wk_ref, wo_ref, g_ref, b_ref, prev_ref,
         o_ref, tail_ref, xb_ref, ue_ref, halo_ref) = refs
    i = pl.program_id(0)
    j = pl.program_id(1)
    tm = x_ref.shape[0]
    tail_rows = tail_ref.shape[0]

    @pl.when(j == 0)
    def _():
        _cast_rows(x_ref, xb_ref, _row_chunk(tm))

    xb = xb_ref[...]
    gate_b = jnp.dot(xb, wb_ref[...].astype(BF16), preferred_element_type=F32)
    gate_c = jnp.dot(xb, wc_ref[...].astype(BF16), preferred_element_type=F32)
    hh = jnp.dot(xb, wh_ref[...].astype(BF16), preferred_element_type=F32)
    u = gate_c * hh

    if multi_seq:
        ue_ref[0:SUBLANES, :] = jnp.zeros((SUBLANES, u.shape[1]), F32)
    else:
        first = (i % tiles_per_seq) == 0

        @pl.when(first)
        def _():
            ue_ref[0:SUBLANES, :] = prev_ref[...]

        @pl.when(jnp.logical_not(first))
        def _():
            ue_ref[0:SUBLANES, :] = halo_ref[j]

    ue_ref[SUBLANES:SUBLANES + tm, :] = u
    s1 = ue_ref[SUBLANES - 1:SUBLANES - 1 + tm, :]
    s2 = ue_ref[SUBLANES - 2:SUBLANES - 2 + tm, :]
    if multi_seq:
        t = lax.broadcasted_iota(jnp.int32, u.shape, 0) % seq_len
        s1 = jnp.where(t >= 1, s1, p1_ref[...])
        s2 = jnp.where(t >= 2, s2, p2_ref[...])
    wk = wk_ref[...]
    y = wk[0:1, :] * s2
    y = y + wk[1:2, :] * s1
    y = y + wk[2:3, :] * u
    yb = (gate_b * y).astype(BF16)
    part = jnp.dot(yb, wo_ref[...].astype(BF16), preferred_element_type=F32)
    _accumulate(o_ref, part, j)

    halo_ref[j] = ue_ref[tm:tm + SUBLANES, :]
    tail_ref[...] = ue_ref[SUBLANES + tm - tail_rows:SUBLANES + tm, :]

    @pl.when(j == n_j - 1)
    def _():
        _post_norm_rows(x_ref, o_ref, g_ref, b_ref, alpha, 1.0, _row_chunk(tm))


def _conv(x, w_in, w_conv, w_out, lng, lnb, layer, ln_idx, prev, *, alpha, tm, tn, seq_len):
    m, d = x.shape
    n_i, n_j = m // tm, d // tn
    multi_seq = seq_len < tm
    tiles_per_seq = max(seq_len // tm, 1)
    tail_rows = tm if multi_seq else SUBLANES
    in_specs = [
        _row_tile_spec(tm, d),
        pl.BlockSpec((None, d, tn), lambda i, j: (layer, 0, j)),
        pl.BlockSpec((None, d, tn), lambda i, j: (layer, 0, n_j + j)),
        pl.BlockSpec((None, d, tn), lambda i, j: (layer, 0, 2 * n_j + j)),
        pl.BlockSpec((None, CONV_W, tn), lambda i, j: (layer, 0, j)),
        pl.BlockSpec((None, tn, d), lambda i, j: (layer, j, 0)),
        pl.BlockSpec((None, 1, d), lambda i, j: (ln_idx, 0, 0)),
        pl.BlockSpec((None, 1, d), lambda i, j: (ln_idx, 0, 0)),
    ]
    if multi_seq:
        in_specs += [pl.BlockSpec((tm, tn), lambda i, j: (i, j))] * 2
        extra = tuple(prev)
    else:
        in_specs += [pl.BlockSpec((None, SUBLANES, tn), lambda i, j: (i // tiles_per_seq, 0, j))]
        extra = (prev,)
    return pl.pallas_call(
        functools.partial(_conv_body, alpha=alpha, n_j=n_j, tiles_per_seq=tiles_per_seq,
                          seq_len=seq_len, multi_seq=multi_seq),
        grid=(n_i, n_j),
        in_specs=in_specs,
        out_specs=[
            pl.BlockSpec((tm, d), lambda i, j: (i, 0)),
            pl.BlockSpec((None, tail_rows, tn), lambda i, j: (i, 0, j)),
        ],
        out_shape=[
            jax.ShapeDtypeStruct((m, d), F32),
            jax.ShapeDtypeStruct((n_i, tail_rows, d), F32),
        ],
        scratch_shapes=[
            pltpu.VMEM((tm, d), BF16),
            pltpu.VMEM((tm + SUBLANES, tn), F32),
            pltpu.VMEM((n_j, SUBLANES, tn), F32),
        ],
        compiler_params=_params("arbitrary", "arbitrary"),
        name="conv",
    )(x, w_in, w_in, w_in, w_conv, w_out, lng, lnb, *extra)


def _rope_cols(y, cos, sin_signed):
    heads = []
    for c in range(y.shape[1] // LANES):
        yh = y[:, c * LANES:(c + 1) * LANES]
        heads.append(yh * cos + pltpu.roll(yh, LANES // 2, axis=1) * sin_signed)
    return heads[0] if len(heads) == 1 else jnp.concatenate(heads, axis=1)


def _proj_body(*refs, rope):
    n_w = len(rope)
    x_ref = refs[0]
    w_refs = refs[1:1 + n_w]
    cos_ref, sin_ref = refs[1 + n_w:3 + n_w]
    o_refs = refs[3 + n_w:3 + 2 * n_w]
    xb_ref = refs[3 + 2 * n_w]
    j = pl.program_id(1)

    @pl.when(j == 0)
    def _():
        _cast_rows(x_ref, xb_ref, _row_chunk(x_ref.shape[0]))

    xb = xb_ref[...]
    for w_ref, o_ref, use_rope in zip(w_refs, o_refs, rope):
        y = jnp.dot(xb, w_ref[...].astype(BF16), preferred_element_type=F32)
        if use_rope:
            y = _rope_cols(y, cos_ref[...], sin_ref[...])
        o_ref[...] = y


def _proj(x, weights, rope, cos, sin_signed, *, tm, tn):
    m, d = x.shape
    n = weights[0][0].shape[-1]
    n_i, n_j = m // tm, n // tn
    pos_tiles = cos.shape[0] // tm
    w_specs = []
    for w, lead in weights:
        if lead is None:
            w_specs.append(pl.BlockSpec((d, tn), lambda i, j: (0, j)))
        else:
            w_specs.append(pl.BlockSpec((None, d, tn), lambda i, j, lead=lead: (lead, 0, j)))
    pos_spec = pl.BlockSpec((tm, LANES), lambda i, j: (i % pos_tiles, 0))
    return pl.pallas_call(
        functools.partial(_proj_body, rope=tuple(rope)),
        grid=(n_i, n_j),
        in_specs=[_row_tile_spec(tm, d)] + w_specs + [pos_spec, pos_spec],
        out_specs=[pl.BlockSpec((tm, tn), lambda i, j: (i, j))] * len(weights),
        out_shape=[jax.ShapeDtypeStruct((m, n), F32)] * len(weights),
        scratch_shapes=[pltpu.VMEM((tm, d), BF16)],
        compiler_params=_params("parallel", "arbitrary"),
        name="proj",
    )(x, *[w for w, _ in weights], cos, sin_signed)


def _oproj_body(x_ref, a_ref, w_ref, g_ref, b_ref, o_ref, *, alpha, n_j):
    j = pl.program_id(1)
    part = jnp.dot(a_ref[...], w_ref[...].astype(BF16), preferred_element_type=F32)
    _accumulate(o_ref, part, j)

    @pl.when(j == n_j - 1)
    def _():
        _post_norm_rows(x_ref, o_ref, g_ref, b_ref, alpha, 1.0, _row_chunk(x_ref.shape[0]))


def _oproj(x, a, w_o, lng, lnb, layer, ln_idx, *, alpha, tm, tk):
    m, d = x.shape
    hd = a.shape[1]
    n_i, n_j = m // tm, hd // tk
    return pl.pallas_call(
        functools.partial(_oproj_body, alpha=alpha, n_j=n_j),
        grid=(n_i, n_j),
        in_specs=[
            _row_tile_spec(tm, d),
            pl.BlockSpec((tm, tk), lambda i, j: (i, j)),
            pl.BlockSpec((None, tk, d), lambda i, j: (layer, j, 0)),
            pl.BlockSpec((None, 1, d), lambda i, j: (ln_idx, 0, 0)),
            pl.BlockSpec((None, 1, d), lambda i, j: (ln_idx, 0, 0)),
        ],
        out_specs=pl.BlockSpec((tm, d), lambda i, j: (i, 0)),
        out_shape=jax.ShapeDtypeStruct((m, d), F32),
        compiler_params=_params("parallel", "arbitrary"),
        name="oproj",
    )(x, a, w_o, lng, lnb)


def _select_blocks(gate_t, n_valid, topk):
    nb = gate_t.shape[0]
    blk = lax.broadcasted_iota(jnp.int32, gate_t.shape, 0)
    g = jnp.where(blk < n_valid, gate_t, NEG_INF)
    sel = jnp.zeros(gate_t.shape, jnp.bool_)
    for s in range(topk):
        best = jnp.max(g, axis=0, keepdims=True)
        idx = jnp.min(jnp.where(g == best, blk, nb), axis=0, keepdims=True)
        hit = blk == idx
        sel = jnp.logical_or(sel, jnp.logical_and(hit, s < n_valid))
        g = jnp.where(hit, -jnp.inf, g)
    return sel


def _flash_update(carry, s, mask, vt_b):
    m, l, acc = carry
    s = jnp.where(mask, s, NEG_INF)
    m_new = jnp.maximum(m, jnp.max(s, axis=0, keepdims=True))
    p = jnp.where(mask, jnp.exp(s - m_new), 0.0)
    corr = jnp.exp(m - m_new)
    l = l * corr + jnp.sum(p, axis=0, keepdims=True)
    acc = acc * corr + jnp.dot(vt_b, p.astype(BF16), preferred_element_type=F32)
    return m_new, l, acc


def _attn_body(q_ref, k_ref, v_ref, o_ref, means_ref, vt_ref, sel_ref, *, scale):
    i = pl.program_id(2)
    blk = MOBA_BLOCK
    nb = k_ref.shape[0] // blk
    dh = k_ref.shape[1]

    @pl.when(i == 0)
    def _():
        for n in range(nb):
            rows = slice(n * blk, (n + 1) * blk)
            means_ref[n:n + 1, :] = jnp.sum(k_ref[rows, :], axis=0, keepdims=True) * (1.0 / blk)
            vt_ref[n] = v_ref[rows, :].T.astype(BF16)

    q = q_ref[...]
    gate_t = lax.dot_general(means_ref[...], q, _NT, precision=lax.Precision.HIGHEST,
                             preferred_element_type=F32)
    sel_ref[...] = _select_blocks(gate_t, i, MOBA_TOPK).astype(F32)
    qb = q.astype(BF16)

    def logits(n):
        kb = k_ref[pl.ds(pl.multiple_of(n * blk, blk), blk), :].astype(BF16)
        return lax.dot_general(kb, qb, _NT, preferred_element_type=F32) * scale

    def past(n, carry):
        mask = sel_ref[pl.ds(n, 1), :] > 0.0
        return _flash_update(carry, logits(n), mask, vt_ref[n])

    init = (jnp.full((1, blk), NEG_INF, F32), jnp.zeros((1, blk), F32), jnp.zeros((dh, blk), F32))
    carry = lax.fori_loop(0, i, past, init)
    key = lax.broadcasted_iota(jnp.int32, (blk, blk), 0)
    qry = lax.broadcasted_iota(jnp.int32, (blk, blk), 1)
    _, l, acc = _flash_update(carry, logits(i), key <= qry, vt_ref[i])
    o_ref[...] = (acc / l).T.astype(o_ref.dtype)


def _moba_prompt(q, k, v, *, n_heads):
    b, t, hd = q.shape
    dh = hd // n_heads
    nb = t // MOBA_BLOCK
    return pl.pallas_call(
        functools.partial(_attn_body, scale=dh ** -0.5),
        grid=(b, n_heads, nb),
        in_specs=[
            pl.BlockSpec((None, MOBA_BLOCK, dh), lambda bi, h, i: (bi, i, h)),
            pl.BlockSpec((None, t, dh), lambda bi, h, i: (bi, 0, h)),
            pl.BlockSpec((None, t, dh), lambda bi, h, i: (bi, 0, h)),
        ],
        out_specs=pl.BlockSpec((None, MOBA_BLOCK, dh), lambda bi, h, i: (bi, i, h)),
        out_shape=jax.ShapeDtypeStruct((b, t, hd), BF16),
        scratch_shapes=[
            pltpu.VMEM((nb, dh), F32),
            pltpu.VMEM((nb, dh, MOBA_BLOCK), BF16),
            pltpu.VMEM((nb, MOBA_BLOCK), F32),
        ],
        compiler_params=_params("parallel", "parallel", "arbitrary"),
        name="moba_prompt",
    )(q, k, v)


def _means_body(pt_ref, page_ref, o_ref, *, pages_per_blk, inv_rows):
    p = pl.program_id(1) % pages_per_blk
    part = jnp.sum(page_ref[...], axis=0)

    @pl.when(p == 0)
    def _():
        o_ref[...] = part

    @pl.when(p > 0)
    def _():
        o_ref[...] += part

    @pl.when(p == pages_per_blk - 1)
    def _():
        o_ref[...] = o_ref[...] * inv_rows


def _cache_means(cache_k, page_table):
    _, page, nh, dh = cache_k.shape
    b, n_pages = page_table.shape
    ppb = MOBA_BLOCK // page
    return pl.pallas_call(
        functools.partial(_means_body, pages_per_blk=ppb, inv_rows=1.0 / MOBA_BLOCK),
        grid_spec=pltpu.PrefetchScalarGridSpec(
            num_scalar_prefetch=1,
            grid=(b, n_pages),
            in_specs=[pl.BlockSpec((None, page, nh, dh), lambda bi, p, pt: (pt[bi, p], 0, 0, 0))],
            out_specs=pl.BlockSpec((None, None, nh, dh), lambda bi, p, pt: (bi, p // ppb, 0, 0)),
        ),
        out_shape=jax.ShapeDtypeStruct((b, n_pages // ppb, nh, dh), F32),
        compiler_params=_params("parallel", "arbitrary"),
        name="cache_means",
    )(page_table, cache_k)


def _sgate_body(means_ref, q_ref, o_ref, *, topk):
    nq = q_ref.shape[0]
    nb, nh, _ = means_ref.shape
    o_ref[...] = jnp.zeros(o_ref.shape, jnp.int32)
    means = means_ref[...]
    blk = lax.broadcasted_iota(jnp.int32, (nb, nh), 0)
    for qi in range(nq):
        g = jnp.sum(means * q_ref[qi][None], axis=-1)
        for s in range(topk):
            best = jnp.max(g, axis=0, keepdims=True)
            idx = jnp.min(jnp.where(g == best, blk, nb), axis=0, keepdims=True)
            o_ref[qi, s:s + 1, 0:nh] = idx
            g = jnp.where(blk == idx, -jnp.inf, g)


def _sample_select(means, q4, *, topk):
    b, nb, nh, dh = means.shape
    nq = q4.shape[1]
    return pl.pallas_call(
        functools.partial(_sgate_body, topk=topk),
        grid=(b,),
        in_specs=[
            pl.BlockSpec((None, nb, nh, dh), lambda bi: (bi, 0, 0, 0)),
            pl.BlockSpec((None, nq, nh, dh), lambda bi: (bi, 0, 0, 0)),
        ],
        out_specs=pl.BlockSpec((None, nq, SUBLANES, LANES), lambda bi: (bi, 0, 0, 0)),
        out_shape=jax.ShapeDtypeStruct((b, nq, SUBLANES, LANES), jnp.int32),
        compiler_params=_params("parallel"),
        name="sample_select",
    )(means, q4)


def _sattn_body(pt_ref, idx_ref, q_ref, kn_ref, vn_ref, ck_ref, cv_ref, o_ref,
                kg_ref, vg_ref, sem, *, nq, topk, page, scale):
    b = pl.program_id(0)
    h = pl.program_id(1)
    nh = pl.num_programs(1)
    ppb = MOBA_BLOCK // page
    n_slots = nq * topk

    def copies(slot, half):
        blk = idx_ref[(b * nh + h) * n_slots + slot]
        pg = pt_ref[b, blk * ppb + half]
        rows = pl.ds(slot * MOBA_BLOCK + half * page, page)
        return (pltpu.make_async_copy(ck_ref.at[pg, :, h, :], kg_ref.at[rows, :], sem.at[0]),
                pltpu.make_async_copy(cv_ref.at[pg, :, h, :], vg_ref.at[rows, :], sem.at[1]))

    for slot in range(n_slots):
        for half in range(ppb):
            for cp in copies(slot, half):
                cp.start()
    for slot in range(n_slots):
        for half in range(ppb):
            for cp in copies(slot, half):
                cp.wait()

    rows = q_ref.shape[0]
    qb = q_ref[...].astype(BF16)
    n_keys = n_slots * MOBA_BLOCK
    s_past = lax.dot_general(qb, kg_ref[...].astype(BF16), _NT, preferred_element_type=F32) * scale
    s_new = lax.dot_general(qb, kn_ref[...].astype(BF16), _NT, preferred_element_type=F32) * scale
    r_past = lax.broadcasted_iota(jnp.int32, (rows, n_keys), 0)
    c_past = lax.broadcasted_iota(jnp.int32, (rows, n_keys), 1)
    lo = r_past * (topk * MOBA_BLOCK)
    m_past = jnp.logical_and(c_past >= lo, c_past < lo + topk * MOBA_BLOCK)
    r_new = lax.broadcasted_iota(jnp.int32, (rows, rows), 0)
    c_new = lax.broadcasted_iota(jnp.int32, (rows, rows), 1)
    m_new = jnp.logical_and(c_new <= r_new, c_new < nq)
    s_past = jnp.where(m_past, s_past, NEG_INF)
    s_new = jnp.where(m_new, s_new, NEG_INF)
    top = jnp.maximum(jnp.max(s_past, axis=1, keepdims=True), jnp.max(s_new, axis=1, keepdims=True))
    p_past = jnp.where(m_past, jnp.exp(s_past - top), 0.0)
    p_new = jnp.where(m_new, jnp.exp(s_new - top), 0.0)
    l = jnp.sum(p_past, axis=1, keepdims=True) + jnp.sum(p_new, axis=1, keepdims=True)
    acc = jnp.dot(p_past.astype(BF16), vg_ref[...].astype(BF16), preferred_element_type=F32)
    acc = acc + jnp.dot(p_new.astype(BF16), vn_ref[...].astype(BF16), preferred_element_type=F32)
    o_ref[...] = (acc / l).astype(o_ref.dtype)


def _moba_sample(q8, k8, v8, cache_k, cache_v, page_table, sel_idx, *, nq, n_heads):
    b, rows, hd = q8.shape
    dh = hd // n_heads
    page = cache_k.shape[1]
    n_keys = nq * MOBA_TOPK * MOBA_BLOCK
    blk3 = pl.BlockSpec((None, rows, dh), lambda bi, h, pt, idx: (bi, 0, h))
    return pl.pallas_call(
        functools.partial(_sattn_body, nq=nq, topk=MOBA_TOPK, page=page, scale=dh ** -0.5),
        grid_spec=pltpu.PrefetchScalarGridSpec(
            num_scalar_prefetch=2,
            grid=(b, n_heads),
            in_specs=[blk3, blk3, blk3,
                      pl.BlockSpec(memory_space=pl.ANY), pl.BlockSpec(memory_space=pl.ANY)],
            out_specs=blk3,
            scratch_shapes=[
                pltpu.VMEM((n_keys, dh), F32),
                pltpu.VMEM((n_keys, dh), F32),
                pltpu.SemaphoreType.DMA((2,)),
            ],
        ),
        out_shape=jax.ShapeDtypeStruct((b, rows, hd), BF16),
        compiler_params=_params("arbitrary", "arbitrary"),
        name="moba_sample",
    )(page_table, sel_idx, q8, k8, v8, cache_k, cache_v)


def _rope_tables(pos, dh):
    half = dh // 2
    inv = ROPE_THETA ** (-2.0 * jnp.arange(half, dtype=F32) / dh)
    ang = pos.astype(F32)[:, None] * inv[None, :]
    cos, sin = jnp.cos(ang), jnp.sin(ang)
    return jnp.concatenate([cos, cos], axis=1), jnp.concatenate([-sin, sin], axis=1)


def _pad_rows(a, rows):
    return jnp.pad(a, ((0, 0), (0, rows - a.shape[1]), (0, 0)))


def _trunk(x, n_seq, seq_len, pos, conv_prev, past, w, *, tm, tf, tn):
    (lng, lnb, w_ffn_gate, w_ffn_up, w_ffn_down, w_conv_in, w_conv, w_conv_out,
     w_k, w_v, w_q, w_o) = w
    depth = w_ffn_gate.shape[0]
    n_a = w_conv_in.shape[0]
    alpha = float((2 * depth) ** 0.25)
    m, d = x.shape
    n_heads = cache_heads = w_k.shape[1] // LANES
    dh = w_k.shape[1] // n_heads
    cos, sin_signed = _rope_tables(pos, dh)
    if past is not None:
        cos, sin_signed = jnp.tile(cos, (n_seq, 1)), jnp.tile(sin_signed, (n_seq, 1))
    ffn = functools.partial(_ffn, alpha=alpha, tm=tm, tf=tf)
    conv_states = []
    for layer in range(depth):
        if layer == n_a:
            k_new, v_new = _proj(x, [(w_k, None), (w_v, None)], (True, False), cos, sin_signed,
                                 tm=min(tm, 512), tn=tn)
            if past is not None:
                means = _cache_means(past[0], past[2])
        x = ffn(x, w_ffn_gate, w_ffn_up, w_ffn_down, lng, lnb, layer, 0, 3 * layer)
        if layer < n_a:
            if seq_len < tm:
                prev = conv_prev[layer]
                p1 = jnp.zeros((n_seq, seq_len, d), F32).at[:, 0].set(prev[:, 1])
                p2 = jnp.zeros((n_seq, seq_len, d), F32).at[:, 0].set(prev[:, 0]).at[:, 1].set(prev[:, 1])
                prev_arg = (p1.reshape(m, d), p2.reshape(m, d))
            else:
                prev_arg = jnp.pad(conv_prev[layer], ((0, 0), (SUBLANES - (CONV_W - 1), 0), (0, 0)))
            x, tail = _conv(x, w_conv_in, w_conv, w_conv_out, lng, lnb, layer, 3 * layer + 1,
                            prev_arg, alpha=alpha, tm=tm, tn=tn, seq_len=seq_len)
            if seq_len < tm:
                st = tail.reshape(n_seq, seq_len, d)[:, seq_len - (CONV_W - 1):]
            else:
                tps = seq_len // tm
                st = tail[tps - 1::tps, SUBLANES - (CONV_W - 1):]
            conv_states.append(st)
        else:
            jb = layer - n_a
            (q,) = _proj(x, [(w_q, jb)], (True,), cos, sin_signed, tm=min(tm, 512), tn=tn)
            if past is None:
                a = _moba_prompt(q.reshape(n_seq, seq_len, -1), k_new.reshape(n_seq, seq_len, -1),
                                 v_new.reshape(n_seq, seq_len, -1), n_heads=n_heads)
                a = a.reshape(m, -1)
            else:
                q3 = q.reshape(n_seq, seq_len, -1)
                sel = _sample_select(means, q3.reshape(n_seq, seq_len, n_heads, dh), topk=MOBA_TOPK)
                sel = sel[:, :, :MOBA_TOPK, :n_heads].transpose(0, 3, 1, 2).reshape(-1)
                a8 = _moba_sample(_pad_rows(q3, SUBLANES),
                                  _pad_rows(k_new.reshape(n_seq, seq_len, -1), SUBLANES),
                                  _pad_rows(v_new.reshape(n_seq, seq_len, -1), SUBLANES),
                                  past[0], past[1], past[2], sel, nq=seq_len, n_heads=n_heads)
                a = a8[:, :seq_len].reshape(m, -1)
            x = _oproj(x, a, w_o, lng, lnb, jb, 3 * layer + 1, alpha=alpha, tm=tm, tk=tn)
        x = ffn(x, w_ffn_gate, w_ffn_up, w_ffn_down, lng, lnb, layer, 1, 3 * layer + 2)
    return x, jnp.stack(conv_states), k_new, v_new


def kernel(x_prompt, x_sample, state_conv, cache_k, cache_v, page_table, ln_g, ln_b, w_ffn_gate, w_ffn_up, w_ffn_down, w_conv_in, w_conv, w_conv_out, w_k, w_v, w_q, w_o):
    bp, tp, d = x_prompt.shape
    bs, ts, _ = x_sample.shape
    n_heads, dh = cache_k.shape[2], cache_k.shape[3]
    n_a = w_conv_in.shape[0]
    past_len = page_table.shape[1] * cache_k.shape[1]
    assert dh == LANES and w_conv.shape[1] == CONV_W
    assert tp % MOBA_BLOCK == 0 and past_len % MOBA_BLOCK == 0 and MOBA_BLOCK % cache_k.shape[1] == 0
    assert CONV_W - 1 <= ts <= SUBLANES and past_len // MOBA_BLOCK >= MOBA_TOPK
    lng = ln_g.reshape(-1, 1, d)
    lnb = ln_b.reshape(-1, 1, d)
    w = (lng, lnb, w_ffn_gate, w_ffn_up, w_ffn_down, w_conv_in, w_conv, w_conv_out, w_k, w_v, w_q, w_o)
    f = w_ffn_gate.shape[-1]
    tf = 256 if f % 256 == 0 else f
    tn = 256 if d % 256 == 0 else d

    conv0 = jnp.zeros((n_a, bp, CONV_W - 1, d), F32)
    tm_p = min(1024, tp)
    y_p, conv_p, k_p, v_p = _trunk(x_prompt.reshape(bp * tp, d), bp, tp, jnp.arange(tp, dtype=jnp.int32),
                                   conv0, None, w, tm=tm_p, tf=tf, tn=tn)
    pos_s = past_len + jnp.arange(ts, dtype=jnp.int32)
    y_s, conv_s, k_s, v_s = _trunk(x_sample.reshape(bs * ts, d), bs, ts, pos_s, state_conv,
                                   (cache_k, cache_v, page_table), w, tm=bs * ts, tf=tf, tn=tn)
    return (y_p.reshape(bp, tp, d), y_s.reshape(bs, ts, d), conv_p, conv_s,
            k_p.reshape(bp, tp, n_heads, dh), v_p.reshape(bp, tp, n_heads, dh),
            k_s.reshape(bs, ts, n_heads, dh), v_s.reshape(bs, ts, n_heads, dh))
```

```python
import functools

import jax
import jax.numpy as jnp
from jax import lax
from jax.experimental import pallas as pl
from jax.experimental.pallas import tpu as pltpu

F32 = jnp.float32
BF16 = jnp.bfloat16

MOBA_BLOCK = 256
MOBA_TOPK = 3
CONV_W = 3
ROPE_THETA = 10000.0
LN_EPS = 1e-5
NEG_INF = -1e30

SUBLANES = 8
LANES = 128
VMEM_LIMIT_BYTES = 60 * 1024 * 1024

_NT = (((1,), (1,)), ((), ()))


def _params(*sem):
    return pltpu.CompilerParams(dimension_semantics=sem, vmem_limit_bytes=VMEM_LIMIT_BYTES)


def _row_tile_spec(tm, d):
    return pl.BlockSpec((tm, d), lambda i, j: (i, 0), pipeline_mode=pl.Buffered(1))


def _layer_norm(z, g, b):
    mu = jnp.mean(z, axis=-1, keepdims=True)
    d = z - mu
    var = jnp.mean(d * d, axis=-1, keepdims=True)
    return d * lax.rsqrt(var + LN_EPS) * g + b


def _cast_rows(src_ref, dst_ref, rows):
    n = src_ref.shape[0] // rows

    def body(c, carry):
        r = pl.ds(pl.multiple_of(c * rows, rows), rows)
        dst_ref[r, :] = src_ref[r, :].astype(dst_ref.dtype)
        return carry

    lax.fori_loop(0, n, body, 0)


def _start_tile(x_ref, xb_ref, o_ref, j):
    @pl.when(j == 0)
    def _():
        if xb_ref is not None:
            _cast_rows(x_ref, xb_ref, _row_chunk(x_ref.shape[0]))
        o_ref[...] = jnp.zeros(o_ref.shape, o_ref.dtype)


def _post_norm_rows(x_ref, o_ref, g_ref, b_ref, alpha, scale, rows):
    n = o_ref.shape[0] // rows
    g = g_ref[...]
    b = b_ref[...]

    def body(c, carry):
        r = pl.ds(pl.multiple_of(c * rows, rows), rows)
        z = alpha * x_ref[r, :] + scale * o_ref[r, :]
        o_ref[r, :] = _layer_norm(z, g, b)
        return carry

    lax.fori_loop(0, n, body, 0)


def _row_chunk(tm):
    return min(tm, 64)


def _ffn_body(x_ref, wg_ref, wu_ref, wd_ref, g_ref, b_ref, o_ref, xb_ref, *, alpha, n_j):
    j = pl.program_id(1)

    _start_tile(x_ref, xb_ref, o_ref, j)
    xb = xb_ref[...]
    gate = jnp.dot(xb, wg_ref[...].astype(BF16), preferred_element_type=F32)
    up = jnp.dot(xb, wu_ref[...].astype(BF16), preferred_element_type=F32)
    h = (jax.nn.silu(gate) * up).astype(BF16)
    o_ref[...] += jnp.dot(h, wd_ref[...].astype(BF16), preferred_element_type=F32)

    @pl.when(j == n_j - 1)
    def _():
        _post_norm_rows(x_ref, o_ref, g_ref, b_ref, alpha, 0.5, _row_chunk(x_ref.shape[0]))


def _ffn(x, w_gate, w_up, w_down, lng, lnb, layer, sub, ln_idx, *, alpha, tm, tf):
    m, d = x.shape
    f = w_gate.shape[-1]
    n_i, n_j = m // tm, f // tf
    return pl.pallas_call(
        functools.partial(_ffn_body, alpha=alpha, n_j=n_j),
        grid=(n_i, n_j),
        in_specs=[
            _row_tile_spec(tm, d),
            pl.BlockSpec((None, None, d, tf), lambda i, j: (layer, sub, 0, j)),
            pl.BlockSpec((None, None, d, tf), lambda i, j: (layer, sub, 0, j)),
            pl.BlockSpec((None, None, tf, d), lambda i, j: (layer, sub, j, 0)),
            pl.BlockSpec((None, 1, d), lambda i, j: (ln_idx, 0, 0)),
            pl.BlockSpec((None, 1, d), lambda i, j: (ln_idx, 0, 0)),
        ],
        out_specs=pl.BlockSpec((tm, d), lambda i, j: (i, 0)),
        out_shape=jax.ShapeDtypeStruct((m, d), F32),
        scratch_shapes=[pltpu.VMEM((tm, d), BF16)],
        compiler_params=_params("parallel", "arbitrary"),
        name="ffn",
    )(x, w_gate, w_up, w_down, lng, lnb)


def _conv_body(*refs, alpha, n_j, tiles_per_seq, seq_len, multi_seq):
    if multi_seq:
        (x_ref, wb_ref, wc_ref, wh_ref, wk_ref, wo_ref, g_ref, b_ref, p1_ref, p2_ref,
         o_ref, tail_ref, xb_ref, halo_ref) = refs
    else:
        (x_ref, wb_ref, wc_ref, wh_ref, wk_ref, wo_ref, g_ref, b_ref, prev_ref,
         o_ref, tail_ref, xb_ref, halo_ref) = refs
    i = pl.program_id(0)
    j = pl.program_id(1)
    tm = x_ref.shape[0]
    tail_rows = tail_ref.shape[0]

    _start_tile(x_ref, xb_ref, o_ref, j)
    if not multi_seq:
        @pl.when((i % tiles_per_seq) == 0)
        def _():
            halo_ref[j] = prev_ref[...]

    xb = xb_ref[...]
    gate_b = jnp.dot(xb, wb_ref[...].astype(BF16), preferred_element_type=F32)
    gate_c = jnp.dot(xb, wc_ref[...].astype(BF16), preferred_element_type=F32)
    whb = wh_ref[...].astype(BF16)
    half = tm // 2 if tm % (4 * SUBLANES) == 0 else tm
    hh = jnp.concatenate([jnp.dot(xb[r:r + half], whb, preferred_element_type=F32)
                          for r in range(0, tm, half)], axis=0)
    u = gate_c * hh
    s1 = pltpu.roll(u, 1, axis=0)
    s2 = pltpu.roll(u, 2, axis=0)
    if multi_seq:
        t = lax.broadcasted_iota(jnp.int32, u.shape, 0) % seq_len
        s1 = jnp.where(t >= 1, s1, p1_ref[...])
        s2 = jnp.where(t >= 2, s2, p2_ref[...])
    else:
        halo = halo_ref[j]
        row = lax.broadcasted_iota(jnp.int32, halo.shape, 0)
        before1 = jnp.broadcast_to(halo[SUBLANES - 1:SUBLANES, :], halo.shape)
        before2 = jnp.broadcast_to(halo[SUBLANES - 2:SUBLANES - 1, :], halo.shape)
        top1 = jnp.where(row == 0, before1, s1[0:SUBLANES, :])
        top2 = jnp.where(row == 0, before2, jnp.where(row == 1, before1, s2[0:SUBLANES, :]))
        s1 = jnp.concatenate([top1, s1[SUBLANES:, :]], axis=0)
        s2 = jnp.concatenate([top2, s2[SUBLANES:, :]], axis=0)
    wk = wk_ref[...]
    y = wk[0:1, :] * s2
    y = y + wk[1:2, :] * s1
    y = y + wk[2:3, :] * u
    yb = (gate_b * y).astype(BF16)
    o_ref[...] += jnp.dot(yb, wo_ref[...].astype(BF16), preferred_element_type=F32)

    halo_ref[j] = u[tm - SUBLANES:, :]
    tail_ref[...] = u[tm - tail_rows:, :]

    @pl.when(j == n_j - 1)
    def _():
        _post_norm_rows(x_ref, o_ref, g_ref, b_ref, alpha, 1.0, _row_chunk(tm))


def _conv(x, w_in, w_conv, w_out, lng, lnb, layer, ln_idx, prev, *, alpha, tm, tn, seq_len):
    m, d = x.shape
    n_i, n_j = m // tm, d // tn
    multi_seq = seq_len < tm
    tiles_per_seq = max(seq_len // tm, 1)
    tail_rows = tm if multi_seq else SUBLANES
    in_specs = [
        _row_tile_spec(tm, d),
        pl.BlockSpec((None, d, tn), lambda i, j: (layer, 0, j)),
        pl.BlockSpec((None, d, tn), lambda i, j: (layer, 0, n_j + j)),
        pl.BlockSpec((None, d, tn), lambda i, j: (layer, 0, 2 * n_j + j)),
        pl.BlockSpec((None, CONV_W, tn), lambda i, j: (layer, 0, j)),
        pl.BlockSpec((None, tn, d), lambda i, j: (layer, j, 0)),
        pl.BlockSpec((None, 1, d), lambda i, j: (ln_idx, 0, 0)),
        pl.BlockSpec((None, 1, d), lambda i, j: (ln_idx, 0, 0)),
    ]
    if multi_seq:
        in_specs += [pl.BlockSpec((tm, tn), lambda i, j: (i, j))] * 2
        extra = tuple(prev)
    else:
        in_specs += [pl.BlockSpec((None, SUBLANES, tn), lambda i, j: (i // tiles_per_seq, 0, j))]
        extra = (prev,)
    return pl.pallas_call(
        functools.partial(_conv_body, alpha=alpha, n_j=n_j, tiles_per_seq=tiles_per_seq,
                          seq_len=seq_len, multi_seq=multi_seq),
        grid=(n_i, n_j),
        in_specs=in_specs,
        out_specs=[
            pl.BlockSpec((tm, d), lambda i, j: (i, 0)),
            pl.BlockSpec((None, tail_rows, tn), lambda i, j: (i, 0, j)),
        ],
        out_shape=[
            jax.ShapeDtypeStruct((m, d), F32),
            jax.ShapeDtypeStruct((n_i, tail_rows, d), F32),
        ],
        scratch_shapes=[
            pltpu.VMEM((tm, d), BF16),
            pltpu.VMEM((n_j, SUBLANES, tn), F32),
        ],
        compiler_params=_params("arbitrary", "arbitrary"),
        name="conv",
    )(x, w_in, w_in, w_in, w_conv, w_out, lng, lnb, *extra)


def _rope_cols(y, cos, sin_signed):
    heads = []
    for c in range(y.shape[1] // LANES):
        yh = y[:, c * LANES:(c + 1) * LANES]
        heads.append(yh * cos + pltpu.roll(yh, LANES // 2, axis=1) * sin_signed)
    return heads[0] if len(heads) == 1 else jnp.concatenate(heads, axis=1)


def _proj_body(*refs, rope):
    n_w = len(rope)
    x_ref = refs[0]
    w_refs = refs[1:1 + n_w]
    cos_ref, sin_ref = refs[1 + n_w:3 + n_w]
    o_refs = refs[3 + n_w:3 + 2 * n_w]
    xb_ref = refs[3 + 2 * n_w]
    j = pl.program_id(1)

    @pl.when(j == 0)
    def _():
        _cast_rows(x_ref, xb_ref, _row_chunk(x_ref.shape[0]))

    xb = xb_ref[...]
    for w_ref, o_ref, use_rope in zip(w_refs, o_refs, rope):
        y = jnp.dot(xb, w_ref[...].astype(BF16), preferred_element_type=F32)
        if use_rope:
            y = _rope_cols(y, cos_ref[...], sin_ref[...])
        o_ref[...] = y


def _proj(x, weights, rope, cos, sin_signed, *, tm, tn):
    m, d = x.shape
    n = weights[0][0].shape[-1]
    n_i, n_j = m // tm, n // tn
    pos_tiles = cos.shape[0] // tm
    w_specs = []
    for w, lead in weights:
        if lead is None:
            w_specs.append(pl.BlockSpec((d, tn), lambda i, j: (0, j)))
        else:
            w_specs.append(pl.BlockSpec((None, d, tn), lambda i, j, lead=lead: (lead, 0, j)))
    pos_spec = pl.BlockSpec((tm, LANES), lambda i, j: (i % pos_tiles, 0))
    return pl.pallas_call(
        functools.partial(_proj_body, rope=tuple(rope)),
        grid=(n_i, n_j),
        in_specs=[_row_tile_spec(tm, d)] + w_specs + [pos_spec, pos_spec],
        out_specs=[pl.BlockSpec((tm, tn), lambda i, j: (i, j))] * len(weights),
        out_shape=[jax.ShapeDtypeStruct((m, n), F32)] * len(weights),
        scratch_shapes=[pltpu.VMEM((tm, d), BF16)],
        compiler_params=_params("parallel", "arbitrary"),
        name="proj",
    )(x, *[w for w, _ in weights], cos, sin_signed)


def _oproj_body(x_ref, a_ref, w_ref, g_ref, b_ref, o_ref, *, alpha, n_j):
    j = pl.program_id(1)
    _start_tile(x_ref, None, o_ref, j)
    o_ref[...] += jnp.dot(a_ref[...], w_ref[...].astype(BF16), preferred_element_type=F32)

    @pl.when(j == n_j - 1)
    def _():
        _post_norm_rows(x_ref, o_ref, g_ref, b_ref, alpha, 1.0, _row_chunk(x_ref.shape[0]))


def _oproj(x, a, w_o, lng, lnb, layer, ln_idx, *, alpha, tm, tk):
    m, d = x.shape
    hd = a.shape[1]
    n_i, n_j = m // tm, hd // tk
    return pl.pallas_call(
        functools.partial(_oproj_body, alpha=alpha, n_j=n_j),
        grid=(n_i, n_j),
        in_specs=[
            _row_tile_spec(tm, d),
            pl.BlockSpec((tm, tk), lambda i, j: (i, j)),
            pl.BlockSpec((None, tk, d), lambda i, j: (layer, j, 0)),
            pl.BlockSpec((None, 1, d), lambda i, j: (ln_idx, 0, 0)),
            pl.BlockSpec((None, 1, d), lambda i, j: (ln_idx, 0, 0)),
        ],
        out_specs=pl.BlockSpec((tm, d), lambda i, j: (i, 0)),
        out_shape=jax.ShapeDtypeStruct((m, d), F32),
        compiler_params=_params("parallel", "arbitrary"),
        name="oproj",
    )(x, a, w_o, lng, lnb)


def _select_blocks(gate_t, n_valid, topk):
    nb = gate_t.shape[0]
    blk = lax.broadcasted_iota(jnp.int32, gate_t.shape, 0)
    g = jnp.where(blk < n_valid, gate_t, NEG_INF)
    sel = jnp.zeros(gate_t.shape, jnp.bool_)
    for s in range(topk):
        best = jnp.max(g, axis=0, keepdims=True)
        idx = jnp.min(jnp.where(g == best, blk, nb), axis=0, keepdims=True)
        hit = blk == idx
        sel = jnp.logical_or(sel, jnp.logical_and(hit, s < n_valid))
        g = jnp.where(hit, -jnp.inf, g)
    return sel


def _attn_body(q_ref, k_ref, v_ref, o_ref, means_ref, kb_ref, vt_ref, bias_ref, *, scale, hps):
    i = pl.program_id(2)
    blk = MOBA_BLOCK
    nb = k_ref.shape[0] // blk
    dh = LANES

    @pl.when(i == 0)
    def _():
        for hh in range(hps):
            cols = slice(hh * dh, (hh + 1) * dh)
            for n in range(nb):
                rows = slice(n * blk, (n + 1) * blk)
                kf = k_ref[rows, cols]
                means_ref[hh, n:n + 1, :] = jnp.sum(kf, axis=0, keepdims=True) * (1.0 / blk)
                kb_ref[hh, n] = kf.astype(BF16)
                vt_ref[hh, n] = v_ref[rows, cols].T.astype(BF16)

    key = lax.broadcasted_iota(jnp.int32, (blk, blk), 0)
    qry = lax.broadcasted_iota(jnp.int32, (blk, blk), 1)
    qbs, carry = [], []
    for hh in range(hps):
        q = q_ref[:, hh * dh:(hh + 1) * dh]
        gate_t = lax.dot_general(means_ref[hh], q, _NT, precision=lax.Precision.HIGHEST,
                                 preferred_element_type=F32)
        bias_ref[hh] = jnp.where(_select_blocks(gate_t, i, MOBA_TOPK), 0.0, NEG_INF)
        qb = (q * scale).astype(BF16)
        s = lax.dot_general(kb_ref[hh, i], qb, _NT, preferred_element_type=F32)
        s = jnp.where(key <= qry, s, NEG_INF)
        m = jnp.max(s, axis=0, keepdims=True)
        p = jnp.exp(s - m)
        l = jnp.sum(p, axis=0, keepdims=True)
        acc = jnp.dot(vt_ref[hh, i], p.astype(BF16), preferred_element_type=F32)
        qbs.append(qb)
        carry += [m, l, acc]

    def past(n, carry):
        ss = [lax.dot_general(kb_ref[hh, n], qbs[hh], _NT, preferred_element_type=F32)
              for hh in range(hps)]
        stats, ps = [], []
        for hh in range(hps):
            m, l = carry[3 * hh:3 * hh + 2]
            s = ss[hh] + bias_ref[hh, pl.ds(n, 1), :]
            m_new = jnp.maximum(m, jnp.max(s, axis=0, keepdims=True))
            p = jnp.exp(s - m_new)
            corr = jnp.exp(m - m_new)
            stats.append((m_new, l * corr + jnp.sum(p, axis=0, keepdims=True), corr))
            ps.append(p.astype(BF16))
        out = []
        for hh in range(hps):
            m_new, l, corr = stats[hh]
            pv = jnp.dot(vt_ref[hh, n], ps[hh], preferred_element_type=F32)
            out += [m_new, l, carry[3 * hh + 2] * corr + pv]
        return tuple(out)

    carry = lax.fori_loop(0, i, past, tuple(carry))
    for hh in range(hps):
        _, l, acc = carry[3 * hh:3 * hh + 3]
        o_ref[:, hh * dh:(hh + 1) * dh] = (acc / l).T.astype(o_ref.dtype)


def _moba_prompt(q, k, v, *, n_heads, hps):
    b, t, hd = q.shape
    dh = hd // n_heads
    nb = t // MOBA_BLOCK
    w = hps * dh
    return pl.pallas_call(
        functools.partial(_attn_body, scale=dh ** -0.5, hps=hps),
        grid=(b, n_heads // hps, nb),
        in_specs=[
            pl.BlockSpec((None, MOBA_BLOCK, w), lambda bi, h, i: (bi, i, h)),
            pl.BlockSpec((None, t, w), lambda bi, h, i: (bi, 0, h)),
            pl.BlockSpec((None, t, w), lambda bi, h, i: (bi, 0, h)),
        ],
        out_specs=pl.BlockSpec((None, MOBA_BLOCK, w), lambda bi, h, i: (bi, i, h)),
        out_shape=jax.ShapeDtypeStruct((b, t, hd), BF16),
        scratch_shapes=[
            pltpu.VMEM((hps, nb, dh), F32),
            pltpu.VMEM((hps, nb, MOBA_BLOCK, dh), BF16),
            pltpu.VMEM((hps, nb, dh, MOBA_BLOCK), BF16),
            pltpu.VMEM((hps, nb, MOBA_BLOCK), F32),
        ],
        compiler_params=_params("parallel", "parallel", "arbitrary"),
        name="moba_prompt",
    )(q, k, v)


def _means_body(pt_ref, *refs, pages_per_blk, inv_rows):
    page_refs, o_ref = refs[:-1], refs[-1]
    for blk in range(len(page_refs) // pages_per_blk):
        total = jnp.sum(page_refs[blk * pages_per_blk][...], axis=0)
        for p in range(1, pages_per_blk):
            total = total + jnp.sum(page_refs[blk * pages_per_blk + p][...], axis=0)
        o_ref[blk] = total * inv_rows


def _cache_means(cache_k, page_table, *, blocks_per_step):
    _, page, nh, dh = cache_k.shape
    b, n_pages = page_table.shape
    ppb = MOBA_BLOCK // page
    pps = ppb * blocks_per_step
    page_specs = [
        pl.BlockSpec((None, page, nh, dh), lambda bi, s, pt, k=k: (pt[bi, s * pps + k], 0, 0, 0))
        for k in range(pps)
    ]
    return pl.pallas_call(
        functools.partial(_means_body, pages_per_blk=ppb, inv_rows=1.0 / MOBA_BLOCK),
        grid_spec=pltpu.PrefetchScalarGridSpec(
            num_scalar_prefetch=1,
            grid=(b, n_pages // pps),
            in_specs=page_specs,
            out_specs=pl.BlockSpec((None, blocks_per_step, nh, dh), lambda bi, s, pt: (bi, s, 0, 0)),
        ),
        out_shape=jax.ShapeDtypeStruct((b, n_pages // ppb, nh, dh), F32),
        compiler_params=_params("parallel", "arbitrary"),
        name="cache_means",
    )(page_table, *([cache_k] * pps))


def _sgate_body(means_ref, q_ref, o_ref, *, topk):
    nq = q_ref.shape[0]
    nb, nh, _ = means_ref.shape
    o_ref[...] = jnp.zeros(o_ref.shape, jnp.int32)
    means = means_ref[...]
    blk = lax.broadcasted_iota(jnp.int32, (nb, nh), 0)
    for qi in range(nq):
        g = jnp.sum(means * q_ref[qi][None], axis=-1)
        for s in range(topk):
            best = jnp.max(g, axis=0, keepdims=True)
            idx = jnp.min(jnp.where(g == best, blk, nb), axis=0, keepdims=True)
            o_ref[qi, s:s + 1, 0:nh] = idx
            g = jnp.where(blk == idx, -jnp.inf, g)


def _sample_select(means, q4, *, topk):
    b, nb, nh, dh = means.shape
    nq = q4.shape[1]
    return pl.pallas_call(
        functools.partial(_sgate_body, topk=topk),
        grid=(b,),
        in_specs=[
            pl.BlockSpec((None, nb, nh, dh), lambda bi: (bi, 0, 0, 0)),
            pl.BlockSpec((None, nq, nh, dh), lambda bi: (bi, 0, 0, 0)),
        ],
        out_specs=pl.BlockSpec((None, nq, SUBLANES, LANES), lambda bi: (bi, 0, 0, 0)),
        out_shape=jax.ShapeDtypeStruct((b, nq, SUBLANES, LANES), jnp.int32),
        compiler_params=_params("parallel"),
        name="sample_select",
    )(means, q4)


def _sattn_body(pt_ref, idx_ref, q_ref, kn_ref, vn_ref, ck_ref, cv_ref, o_ref,
                kg_ref, vg_ref, sem, *, nq, topk, page, scale):
    b = pl.program_id(0)
    h = pl.program_id(1)
    nh = pl.num_programs(1)
    ppb = MOBA_BLOCK // page
    n_slots = nq * topk

    def copies(slot, half):
        blk = idx_ref[(b * nh + h) * n_slots + slot]
        pg = pt_ref[b, blk * ppb + half]
        rows = pl.ds(slot * MOBA_BLOCK + half * page, page)
        return (pltpu.make_async_copy(ck_ref.at[pg, :, h, :], kg_ref.at[rows, :], sem.at[0]),
                pltpu.make_async_copy(cv_ref.at[pg, :, h, :], vg_ref.at[rows, :], sem.at[1]))

    for slot in range(n_slots):
        for half in range(ppb):
            for cp in copies(slot, half):
                cp.start()
    for slot in range(n_slots):
        for half in range(ppb):
            for cp in copies(slot, half):
                cp.wait()

    rows = q_ref.shape[0]
    qb = q_ref[...].astype(BF16)
    n_keys = n_slots * MOBA_BLOCK
    s_past = lax.dot_general(qb, kg_ref[...].astype(BF16), _NT, preferred_element_type=F32) * scale
    s_new = lax.dot_general(qb, kn_ref[...].astype(BF16), _NT, preferred_element_type=F32) * scale
    r_past = lax.broadcasted_iota(jnp.int32, (rows, n_keys), 0)
    c_past = lax.broadcasted_iota(jnp.int32, (rows, n_keys), 1)
    lo = r_past * (topk * MOBA_BLOCK)
    m_past = jnp.logical_and(c_past >= lo, c_past < lo + topk * MOBA_BLOCK)
    r_new = lax.broadcasted_iota(jnp.int32, (rows, rows), 0)
    c_new = lax.broadcasted_iota(jnp.int32, (rows, rows), 1)
    m_new = jnp.logical_and(c_new <= r_new, c_new < nq)
    s_past = jnp.where(m_past, s_past, NEG_INF)
    s_new = jnp.where(m_new, s_new, NEG_INF)
    top = jnp.maximum(jnp.max(s_past, axis=1, keepdims=True), jnp.max(s_new, axis=1, keepdims=True))
    p_past = jnp.where(m_past, jnp.exp(s_past - top), 0.0)
    p_new = jnp.where(m_new, jnp.exp(s_new - top), 0.0)
    l = jnp.sum(p_past, axis=1, keepdims=True) + jnp.sum(p_new, axis=1, keepdims=True)
    acc = jnp.dot(p_past.astype(BF16), vg_ref[...].astype(BF16), preferred_element_type=F32)
    acc = acc + jnp.dot(p_new.astype(BF16), vn_ref[...].astype(BF16), preferred_element_type=F32)
    o_ref[...] = (acc / l).astype(o_ref.dtype)


def _moba_sample(q8, k8, v8, cache_k, cache_v, page_table, sel_idx, *, nq, n_heads):
    b, rows, hd = q8.shape
    dh = hd // n_heads
    page = cache_k.shape[1]
    n_keys = nq * MOBA_TOPK * MOBA_BLOCK
    blk3 = pl.BlockSpec((None, rows, dh), lambda bi, h, pt, idx: (bi, 0, h))
    return pl.pallas_call(
        functools.partial(_sattn_body, nq=nq, topk=MOBA_TOPK, page=page, scale=dh ** -0.5),
        grid_spec=pltpu.PrefetchScalarGridSpec(
            num_scalar_prefetch=2,
            grid=(b, n_heads),
            in_specs=[blk3, blk3, blk3,
                      pl.BlockSpec(memory_space=pl.ANY), pl.BlockSpec(memory_space=pl.ANY)],
            out_specs=blk3,
            scratch_shapes=[
                pltpu.VMEM((n_keys, dh), F32),
                pltpu.VMEM((n_keys, dh), F32),
                pltpu.SemaphoreType.DMA((2,)),
            ],
        ),
        out_shape=jax.ShapeDtypeStruct((b, rows, hd), BF16),
        compiler_params=_params("arbitrary", "arbitrary"),
        name="moba_sample",
    )(page_table, sel_idx, q8, k8, v8, cache_k, cache_v)


def _rope_tables(pos, dh):
    half = dh // 2
    inv = ROPE_THETA ** (-2.0 * jnp.arange(half, dtype=F32) / dh)
    ang = pos.astype(F32)[:, None] * inv[None, :]
    cos, sin = jnp.cos(ang), jnp.sin(ang)
    return jnp.concatenate([cos, cos], axis=1), jnp.concatenate([-sin, sin], axis=1)


def _pad_rows(a, rows):
    return jnp.pad(a, ((0, 0), (0, rows - a.shape[1]), (0, 0)))


def _trunk(x, n_seq, seq_len, pos, conv_prev, past, w, *, tm, tf, tn):
    (lng, lnb, w_ffn_gate, w_ffn_up, w_ffn_down, w_conv_in, w_conv, w_conv_out,
     w_k, w_v, w_q, w_o) = w
    depth = w_ffn_gate.shape[0]
    n_a = w_conv_in.shape[0]
    alpha = float((2 * depth) ** 0.25)
    m, d = x.shape
    n_heads = w_k.shape[1] // LANES
    dh = w_k.shape[1] // n_heads
    cos, sin_signed = _rope_tables(pos, dh)
    if past is not None:
        cos, sin_signed = jnp.tile(cos, (n_seq, 1)), jnp.tile(sin_signed, (n_seq, 1))
    ffn = functools.partial(_ffn, alpha=alpha, tm=tm, tf=tf)
    wide = 2 * tn if w_k.shape[1] % (2 * tn) == 0 else tn
    conv_states = []
    for layer in range(depth):
        if layer == n_a:
            k_new, v_new = _proj(x, [(w_k, None), (w_v, None)], (True, False), cos, sin_signed,
                                 tm=tm, tn=wide)
            if past is not None:
                n_blocks = past[2].shape[1] * past[0].shape[1] // MOBA_BLOCK
                means = _cache_means(past[0], past[2], blocks_per_step=2 if n_blocks % 2 == 0 else 1)
        x = ffn(x, w_ffn_gate, w_ffn_up, w_ffn_down, lng, lnb, layer, 0, 3 * layer)
        if layer < n_a:
            if seq_len < tm:
                prev = conv_prev[layer]
                p1 = jnp.zeros((n_seq, seq_len, d), F32).at[:, 0].set(prev[:, 1])
                p2 = jnp.zeros((n_seq, seq_len, d), F32).at[:, 0].set(prev[:, 0]).at[:, 1].set(prev[:, 1])
                prev_arg = (p1.reshape(m, d), p2.reshape(m, d))
            else:
                prev_arg = jnp.pad(conv_prev[layer], ((0, 0), (SUBLANES - (CONV_W - 1), 0), (0, 0)))
            x, tail = _conv(x, w_conv_in, w_conv, w_conv_out, lng, lnb, layer, 3 * layer + 1,
                            prev_arg, alpha=alpha, tm=tm, tn=tn, seq_len=seq_len)
            if seq_len < tm:
                st = tail.reshape(n_seq, seq_len, d)[:, seq_len - (CONV_W - 1):]
            else:
                tps = seq_len // tm
                st = tail[tps - 1::tps, SUBLANES - (CONV_W - 1):]
            conv_states.append(st)
        else:
            jb = layer - n_a
            (q,) = _proj(x, [(w_q, jb)], (True,), cos, sin_signed, tm=tm, tn=wide)
            if past is None:
                a = _moba_prompt(q.reshape(n_seq, seq_len, -1), k_new.reshape(n_seq, seq_len, -1),
                                 v_new.reshape(n_seq, seq_len, -1), n_heads=n_heads,
                                 hps=4 if n_heads % 4 == 0 else 1)
                a = a.reshape(m, -1)
            else:
                q3 = q.reshape(n_seq, seq_len, -1)
                sel = _sample_select(means, q3.reshape(n_seq, seq_len, n_heads, dh), topk=MOBA_TOPK)
                sel = sel[:, :, :MOBA_TOPK, :n_heads].transpose(0, 3, 1, 2).reshape(-1)
                a8 = _moba_sample(_pad_rows(q3, SUBLANES),
                                  _pad_rows(k_new.reshape(n_seq, seq_len, -1), SUBLANES),
                                  _pad_rows(v_new.reshape(n_seq, seq_len, -1), SUBLANES),
                                  past[0], past[1], past[2], sel, nq=seq_len, n_heads=n_heads)
                a = a8[:, :seq_len].reshape(m, -1)
            x = _oproj(x, a, w_o, lng, lnb, jb, 3 * layer + 1, alpha=alpha, tm=tm, tk=wide)
        x = ffn(x, w_ffn_gate, w_ffn_up, w_ffn_down, lng, lnb, layer, 1, 3 * layer + 2)
    return x, jnp.stack(conv_states), k_new, v_new


def kernel(x_prompt, x_sample, state_conv, cache_k, cache_v, page_table, ln_g, ln_b, w_ffn_gate, w_ffn_up, w_ffn_down, w_conv_in, w_conv, w_conv_out, w_k, w_v, w_q, w_o):
    bp, tp, d = x_prompt.shape
    bs, ts, _ = x_sample.shape
    n_heads, dh = cache_k.shape[2], cache_k.shape[3]
    n_a = w_conv_in.shape[0]
    past_len = page_table.shape[1] * cache_k.shape[1]
    assert dh == LANES and w_conv.shape[1] == CONV_W
    assert tp % MOBA_BLOCK == 0 and past_len % MOBA_BLOCK == 0 and MOBA_BLOCK % cache_k.shape[1] == 0
    assert CONV_W - 1 <= ts <= SUBLANES and past_len // MOBA_BLOCK >= MOBA_TOPK
    lng = ln_g.reshape(-1, 1, d)
    lnb = ln_b.reshape(-1, 1, d)
    w = (lng, lnb, w_ffn_gate, w_ffn_up, w_ffn_down, w_conv_in, w_conv, w_conv_out, w_k, w_v, w_q, w_o)
    f = w_ffn_gate.shape[-1]
    tf = 256 if f % 256 == 0 else f
    tn = 256 if d % 256 == 0 else d

    conv0 = jnp.zeros((n_a, bp, CONV_W - 1, d), F32)
    tm_p = min(1024, tp)
    y_p, conv_p, k_p, v_p = _trunk(x_prompt.reshape(bp * tp, d), bp, tp, jnp.arange(tp, dtype=jnp.int32),
                                   conv0, None, w, tm=tm_p, tf=tf, tn=tn)
    pos_s = past_len + jnp.arange(ts, dtype=jnp.int32)
    y_s, conv_s, k_s, v_s = _trunk(x_sample.reshape(bs * ts, d), bs, ts, pos_s, state_conv,
                                   (cache_k, cache_v, page_table), w, tm=bs * ts, tf=tf, tn=tn)
    return (y_p.reshape(bp, tp, d), y_s.reshape(bs, ts, d), conv_p, conv_s,
            k_p.reshape(bp, tp, n_heads, dh), v_p.reshape(bp, tp, n_heads, dh),
            k_s.reshape(bs, ts, n_heads, dh), v_s.reshape(bs, ts, n_heads, dh))
```

```python
import functools

import jax
import jax.numpy as jnp
from jax import lax
from jax.experimental import pallas as pl
from jax.experimental.pallas import tpu as pltpu

F32 = jnp.float32
BF16 = jnp.bfloat16

MOBA_BLOCK = 256
MOBA_TOPK = 3
CONV_W = 3
ROPE_THETA = 10000.0
LN_EPS = 1e-5
NEG_INF = -1e30

SUBLANES = 8
LANES = 128
VMEM_LIMIT_BYTES = 60 * 1024 * 1024

_NT = (((1,), (1,)), ((), ()))


def _params(*sem):
    return pltpu.CompilerParams(dimension_semantics=sem, vmem_limit_bytes=VMEM_LIMIT_BYTES)


def _row_tile_spec(tm, d):
    return pl.BlockSpec((tm, d), lambda i, j: (i, 0), pipeline_mode=pl.Buffered(1))


def _layer_norm(z, g, b):
    mu = jnp.mean(z, axis=-1, keepdims=True)
    d = z - mu
    var = jnp.mean(d * d, axis=-1, keepdims=True)
    return d * lax.rsqrt(var + LN_EPS) * g + b


def _cast_rows(src_ref, dst_ref, rows):
    n = src_ref.shape[0] // rows

    def body(c, carry):
        r = pl.ds(pl.multiple_of(c * rows, rows), rows)
        dst_ref[r, :] = src_ref[r, :].astype(dst_ref.dtype)
        return carry

    lax.fori_loop(0, n, body, 0)


def _start_tile(x_ref, xb_ref, o_ref, j):
    @pl.when(j == 0)
    def _():
        if xb_ref is not None:
            _cast_rows(x_ref, xb_ref, _row_chunk(x_ref.shape[0]))
        o_ref[...] = jnp.zeros(o_ref.shape, o_ref.dtype)


def _post_norm_rows(x_ref, o_ref, g_ref, b_ref, alpha, scale, rows):
    n = o_ref.shape[0] // rows
    g = g_ref[...]
    b = b_ref[...]

    def body(c, carry):
        r = pl.ds(pl.multiple_of(c * rows, rows), rows)
        z = alpha * x_ref[r, :] + scale * o_ref[r, :]
        o_ref[r, :] = _layer_norm(z, g, b)
        return carry

    lax.fori_loop(0, n, body, 0)


def _row_chunk(tm):
    return min(tm, 64)


def _ffn_tile(x_ref, wg_ref, wu_ref, wd_ref, g_ref, b_ref, o_ref, xb_ref, j, *, alpha, n_j):
    _start_tile(x_ref, xb_ref, o_ref, j)
    xb = xb_ref[...]
    gate = jnp.dot(xb, wg_ref[...].astype(BF16), preferred_element_type=F32)
    up = jnp.dot(xb, wu_ref[...].astype(BF16), preferred_element_type=F32)
    h = (jax.nn.silu(gate) * up).astype(BF16)
    o_ref[...] += jnp.dot(h, wd_ref[...].astype(BF16), preferred_element_type=F32)

    @pl.when(j == n_j - 1)
    def _():
        _post_norm_rows(x_ref, o_ref, g_ref, b_ref, alpha, 0.5, _row_chunk(x_ref.shape[0]))


def _ffn_body(xp_ref, xs_ref, wg_ref, wu_ref, wd_ref, g_ref, b_ref, op_ref, os_ref,
              xbp_ref, xbs_ref, *, n_p, **kw):
    i = pl.program_id(0)
    j = pl.program_id(1)

    @pl.when(i < n_p)
    def _():
        _ffn_tile(xp_ref, wg_ref, wu_ref, wd_ref, g_ref, b_ref, op_ref, xbp_ref, j, **kw)

    @pl.when(i == n_p)
    def _():
        _ffn_tile(xs_ref, wg_ref, wu_ref, wd_ref, g_ref, b_ref, os_ref, xbs_ref, j, **kw)


def _ffn(xp, xs, w_gate, w_up, w_down, lng, lnb, layer, sub, ln_idx, *, alpha, tm, tf):
    mp, d = xp.shape
    ms = xs.shape[0]
    f = w_gate.shape[-1]
    n_p, n_j = mp // tm, f // tf
    prompt_tile = pl.BlockSpec((tm, d), lambda i, j: (jnp.minimum(i, n_p - 1), 0))
    sample_tile = pl.BlockSpec((ms, d), lambda i, j: (0, 0))
    return pl.pallas_call(
        functools.partial(_ffn_body, alpha=alpha, n_j=n_j, n_p=n_p),
        grid=(n_p + 1, n_j),
        in_specs=[
            prompt_tile,
            sample_tile,
            pl.BlockSpec((None, None, d, tf), lambda i, j: (layer, sub, 0, j)),
            pl.BlockSpec((None, None, d, tf), lambda i, j: (layer, sub, 0, j)),
            pl.BlockSpec((None, None, tf, d), lambda i, j: (layer, sub, j, 0)),
            pl.BlockSpec((None, 1, d), lambda i, j: (ln_idx, 0, 0)),
            pl.BlockSpec((None, 1, d), lambda i, j: (ln_idx, 0, 0)),
        ],
        out_specs=[prompt_tile, sample_tile],
        out_shape=[jax.ShapeDtypeStruct((mp, d), F32), jax.ShapeDtypeStruct((ms, d), F32)],
        scratch_shapes=[pltpu.VMEM((tm, d), BF16), pltpu.VMEM((ms, d), BF16)],
        compiler_params=_params("arbitrary", "arbitrary"),
        name="ffn",
    )(xp, xs, w_gate, w_up, w_down, lng, lnb)


def _conv_body(*refs, alpha, n_j, tiles_per_seq, seq_len, multi_seq):
    if multi_seq:
        (x_ref, wb_ref, wc_ref, wh_ref, wk_ref, wo_ref, g_ref, b_ref, p1_ref, p2_ref,
         o_ref, tail_ref, xb_ref, halo_ref) = refs
    else:
        (x_ref, wb_ref, wc_ref, wh_ref, wk_ref, wo_ref, g_ref, b_ref, prev_ref,
         o_ref, tail_ref, xb_ref, halo_ref) = refs
    i = pl.program_id(0)
    j = pl.program_id(1)
    tm = x_ref.shape[0]
    tail_rows = tail_ref.shape[0]

    _start_tile(x_ref, xb_ref, o_ref, j)
    if not multi_seq:
        @pl.when((i % tiles_per_seq) == 0)
        def _():
            halo_ref[j] = prev_ref[...]

    xb = xb_ref[...]
    gate_b = jnp.dot(xb, wb_ref[...].astype(BF16), preferred_element_type=F32)
    gate_c = jnp.dot(xb, wc_ref[...].astype(BF16), preferred_element_type=F32)
    whb = wh_ref[...].astype(BF16)
    half = tm // 2 if tm % (4 * SUBLANES) == 0 else tm
    hh = jnp.concatenate([jnp.dot(xb[r:r + half], whb, preferred_element_type=F32)
                          for r in range(0, tm, half)], axis=0)
    u = gate_c * hh
    s1 = pltpu.roll(u, 1, axis=0)
    s2 = pltpu.roll(u, 2, axis=0)
    if multi_seq:
        t = lax.broadcasted_iota(jnp.int32, u.shape, 0) % seq_len
        s1 = jnp.where(t >= 1, s1, p1_ref[...])
        s2 = jnp.where(t >= 2, s2, p2_ref[...])
    else:
        halo = halo_ref[j]
        row = lax.broadcasted_iota(jnp.int32, halo.shape, 0)
        before1 = jnp.broadcast_to(halo[SUBLANES - 1:SUBLANES, :], halo.shape)
        before2 = jnp.broadcast_to(halo[SUBLANES - 2:SUBLANES - 1, :], halo.shape)
        top1 = jnp.where(row == 0, before1, s1[0:SUBLANES, :])
        top2 = jnp.where(row == 0, before2, jnp.where(row == 1, before1, s2[0:SUBLANES, :]))
        s1 = jnp.concatenate([top1, s1[SUBLANES:, :]], axis=0)
        s2 = jnp.concatenate([top2, s2[SUBLANES:, :]], axis=0)
    wk = wk_ref[...]
    y = wk[0:1, :] * s2
    y = y + wk[1:2, :] * s1
    y = y + wk[2:3, :] * u
    yb = (gate_b * y).astype(BF16)
    o_ref[...] += jnp.dot(yb, wo_ref[...].astype(BF16), preferred_element_type=F32)

    halo_ref[j] = u[tm - SUBLANES:, :]
    tail_ref[...] = u[tm - tail_rows:, :]

    @pl.when(j == n_j - 1)
    def _():
        _post_norm_rows(x_ref, o_ref, g_ref, b_ref, alpha, 1.0, _row_chunk(tm))


def _conv(x, w_in, w_conv, w_out, lng, lnb, layer, ln_idx, prev, *, alpha, tm, tn, seq_len):
    m, d = x.shape
    n_i, n_j = m // tm, d // tn
    multi_seq = seq_len < tm
    tiles_per_seq = max(seq_len // tm, 1)
    tail_rows = tm if multi_seq else SUBLANES
    in_specs = [
        _row_tile_spec(tm, d),
        pl.BlockSpec((None, d, tn), lambda i, j: (layer, 0, j)),
        pl.BlockSpec((None, d, tn), lambda i, j: (layer, 0, n_j + j)),
        pl.BlockSpec((None, d, tn), lambda i, j: (layer, 0, 2 * n_j + j)),
        pl.BlockSpec((None, CONV_W, tn), lambda i, j: (layer, 0, j)),
        pl.BlockSpec((None, tn, d), lambda i, j: (layer, j, 0)),
        pl.BlockSpec((None, 1, d), lambda i, j: (ln_idx, 0, 0)),
        pl.BlockSpec((None, 1, d), lambda i, j: (ln_idx, 0, 0)),
    ]
    if multi_seq:
        in_specs += [pl.BlockSpec((tm, tn), lambda i, j: (i, j))] * 2
        extra = tuple(prev)
    else:
        in_specs += [pl.BlockSpec((None, SUBLANES, tn), lambda i, j: (i // tiles_per_seq, 0, j))]
        extra = (prev,)
    return pl.pallas_call(
        functools.partial(_conv_body, alpha=alpha, n_j=n_j, tiles_per_seq=tiles_per_seq,
                          seq_len=seq_len, multi_seq=multi_seq),
        grid=(n_i, n_j),
        in_specs=in_specs,
        out_specs=[
            pl.BlockSpec((tm, d), lambda i, j: (i, 0)),
            pl.BlockSpec((None, tail_rows, tn), lambda i, j: (i, 0, j)),
        ],
        out_shape=[
            jax.ShapeDtypeStruct((m, d), F32),
            jax.ShapeDtypeStruct((n_i, tail_rows, d), F32),
        ],
        scratch_shapes=[
            pltpu.VMEM((tm, d), BF16),
            pltpu.VMEM((n_j, SUBLANES, tn), F32),
        ],
        compiler_params=_params("arbitrary", "arbitrary"),
        name="conv",
    )(x, w_in, w_in, w_in, w_conv, w_out, lng, lnb, *extra)


def _rope_cols(y, cos, sin_signed):
    heads = []
    for c in range(y.shape[1] // LANES):
        yh = y[:, c * LANES:(c + 1) * LANES]
        heads.append(yh * cos + pltpu.roll(yh, LANES // 2, axis=1) * sin_signed)
    return heads[0] if len(heads) == 1 else jnp.concatenate(heads, axis=1)


def _proj_body(*refs, rope):
    n_w = len(rope)
    x_ref = refs[0]
    w_refs = refs[1:1 + n_w]
    cos_ref, sin_ref = refs[1 + n_w:3 + n_w]
    o_refs = refs[3 + n_w:3 + 2 * n_w]
    xb_ref = refs[3 + 2 * n_w]
    j = pl.program_id(1)

    @pl.when(j == 0)
    def _():
        _cast_rows(x_ref, xb_ref, _row_chunk(x_ref.shape[0]))

    xb = xb_ref[...]
    for w_ref, o_ref, use_rope in zip(w_refs, o_refs, rope):
        y = jnp.dot(xb, w_ref[...].astype(BF16), preferred_element_type=F32)
        if use_rope:
            y = _rope_cols(y, cos_ref[...], sin_ref[...])
        o_ref[...] = y


def _proj(x, weights, rope, cos, sin_signed, *, tm, tn):
    m, d = x.shape
    n = weights[0][0].shape[-1]
    n_i, n_j = m // tm, n // tn
    pos_tiles = cos.shape[0] // tm
    w_specs = []
    for w, lead in weights:
        if lead is None:
            w_specs.append(pl.BlockSpec((d, tn), lambda i, j: (0, j)))
        else:
            w_specs.append(pl.BlockSpec((None, d, tn), lambda i, j, lead=lead: (lead, 0, j)))
    pos_spec = pl.BlockSpec((tm, LANES), lambda i, j: (i % pos_tiles, 0))
    return pl.pallas_call(
        functools.partial(_proj_body, rope=tuple(rope)),
        grid=(n_i, n_j),
        in_specs=[pl.BlockSpec((tm, d), lambda i, j: (i, 0))] + w_specs + [pos_spec, pos_spec],
        out_specs=[pl.BlockSpec((tm, tn), lambda i, j: (i, j))] * len(weights),
        out_shape=[jax.ShapeDtypeStruct((m, n), F32)] * len(weights),
        scratch_shapes=[pltpu.VMEM((tm, d), BF16)],
        compiler_params=_params("parallel", "arbitrary"),
        name="proj",
    )(x, *[w for w, _ in weights], cos, sin_signed)


def _oproj_body(x_ref, a_ref, w_ref, g_ref, b_ref, o_ref, *, alpha, n_j):
    j = pl.program_id(1)
    _start_tile(x_ref, None, o_ref, j)
    o_ref[...] += jnp.dot(a_ref[...], w_ref[...].astype(BF16), preferred_element_type=F32)

    @pl.when(j == n_j - 1)
    def _():
        _post_norm_rows(x_ref, o_ref, g_ref, b_ref, alpha, 1.0, _row_chunk(x_ref.shape[0]))


def _oproj(x, a, w_o, lng, lnb, layer, ln_idx, *, alpha, tm, tk):
    m, d = x.shape
    hd = a.shape[1]
    n_i, n_j = m // tm, hd // tk
    return pl.pallas_call(
        functools.partial(_oproj_body, alpha=alpha, n_j=n_j),
        grid=(n_i, n_j),
        in_specs=[
            pl.BlockSpec((tm, d), lambda i, j: (i, 0)),
            pl.BlockSpec((tm, tk), lambda i, j: (i, j)),
            pl.BlockSpec((None, tk, d), lambda i, j: (layer, j, 0)),
            pl.BlockSpec((None, 1, d), lambda i, j: (ln_idx, 0, 0)),
            pl.BlockSpec((None, 1, d), lambda i, j: (ln_idx, 0, 0)),
        ],
        out_specs=pl.BlockSpec((tm, d), lambda i, j: (i, 0)),
        out_shape=jax.ShapeDtypeStruct((m, d), F32),
        compiler_params=_params("parallel", "arbitrary"),
        name="oproj",
    )(x, a, w_o, lng, lnb)


def _select_blocks(gate_t, n_valid, topk):
    nb = gate_t.shape[0]
    blk = lax.broadcasted_iota(jnp.int32, gate_t.shape, 0)
    g = jnp.where(blk < n_valid, gate_t, NEG_INF)
    sel = jnp.zeros(gate_t.shape, jnp.bool_)
    for s in range(topk):
        best = jnp.max(g, axis=0, keepdims=True)
        idx = jnp.min(jnp.where(g == best, blk, nb), axis=0, keepdims=True)
        hit = blk == idx
        sel = jnp.logical_or(sel, jnp.logical_and(hit, s < n_valid))
        g = jnp.where(hit, -jnp.inf, g)
    return sel


def _attn_body(q_ref, k_ref, v_ref, o_ref, means_ref, kb_ref, vt_ref, bias_ref, s_ref, *, scale, hps):
    i = pl.program_id(2)
    blk = MOBA_BLOCK
    nb = k_ref.shape[0] // blk
    dh = LANES

    @pl.when(i == 0)
    def _():
        for hh in range(hps):
            cols = slice(hh * dh, (hh + 1) * dh)
            for n in range(nb):
                rows = slice(n * blk, (n + 1) * blk)
                kf = k_ref[rows, cols]
                means_ref[hh, n:n + 1, :] = jnp.sum(kf, axis=0, keepdims=True) * (1.0 / blk)
                kb_ref[hh, n] = kf.astype(BF16)
                vt_ref[hh, n] = v_ref[rows, cols].T.astype(BF16)

    key = lax.broadcasted_iota(jnp.int32, (blk, blk), 0)
    qry = lax.broadcasted_iota(jnp.int32, (blk, blk), 1)
    qs = [q_ref[:, hh * dh:(hh + 1) * dh] for hh in range(hps)]
    qbs = [(q * scale).astype(BF16) for q in qs]

    def logits(n):
        return [lax.dot_general(kb_ref[hh, n], qbs[hh], _NT, preferred_element_type=F32)
                for hh in range(hps)]

    own, first_past = logits(i), logits(0)
    for hh in range(hps):
        gate_t = lax.dot_general(means_ref[hh], qs[hh], _NT, precision=lax.Precision.HIGHEST,
                                 preferred_element_type=F32)
        bias_ref[hh] = jnp.where(_select_blocks(gate_t, i, MOBA_TOPK), 0.0, NEG_INF)
    carry = []
    for hh in range(hps):
        s = jnp.where(key <= qry, own[hh], NEG_INF)
        m = jnp.max(s, axis=0, keepdims=True)
        p = jnp.exp(s - m)
        l = jnp.sum(p, axis=0, keepdims=True)
        acc = jnp.dot(vt_ref[hh, i], p.astype(BF16), preferred_element_type=F32)
        carry += [m, l, acc]

    def past(n, carry):
        cur = lax.rem(n, 2)
        ss_next = logits(jnp.minimum(n + 1, nb - 1))
        stats, ps = [], []
        for hh in range(hps):
            m, l = carry[3 * hh:3 * hh + 2]
            s = s_ref[cur, hh] + bias_ref[hh, pl.ds(n, 1), :]
            m_new = jnp.maximum(m, jnp.max(s, axis=0, keepdims=True))
            p = jnp.exp(s - m_new)
            corr = jnp.exp(m - m_new)
            stats.append((m_new, l * corr + jnp.sum(p, axis=0, keepdims=True), corr))
            ps.append(p.astype(BF16))
        out = []
        for hh in range(hps):
            m_new, l, corr = stats[hh]
            pv = jnp.dot(vt_ref[hh, n], ps[hh], preferred_element_type=F32)
            out += [m_new, l, carry[3 * hh + 2] * corr + pv]
        for hh in range(hps):
            s_ref[1 - cur, hh] = ss_next[hh]
        return tuple(out)

    for hh in range(hps):
        s_ref[0, hh] = first_past[hh]
    carry = lax.fori_loop(0, i, past, tuple(carry))
    for hh in range(hps):
        _, l, acc = carry[3 * hh:3 * hh + 3]
        o_ref[:, hh * dh:(hh + 1) * dh] = (acc / l).T.astype(o_ref.dtype)


def _moba_prompt(q, k, v, *, n_heads, hps):
    b, t, hd = q.shape
    dh = hd // n_heads
    nb = t // MOBA_BLOCK
    w = hps * dh
    return pl.pallas_call(
        functools.partial(_attn_body, scale=dh ** -0.5, hps=hps),
        grid=(b, n_heads // hps, nb),
        in_specs=[
            pl.BlockSpec((None, MOBA_BLOCK, w), lambda bi, h, i: (bi, i, h)),
            pl.BlockSpec((None, t, w), lambda bi, h, i: (bi, 0, h)),
            pl.BlockSpec((None, t, w), lambda bi, h, i: (bi, 0, h)),
        ],
        out_specs=pl.BlockSpec((None, MOBA_BLOCK, w), lambda bi, h, i: (bi, i, h)),
        out_shape=jax.ShapeDtypeStruct((b, t, hd), BF16),
        scratch_shapes=[
            pltpu.VMEM((hps, nb, dh), F32),
            pltpu.VMEM((hps, nb, MOBA_BLOCK, dh), BF16),
            pltpu.VMEM((hps, nb, dh, MOBA_BLOCK), BF16),
            pltpu.VMEM((hps, nb, MOBA_BLOCK), F32),
            pltpu.VMEM((2, hps, MOBA_BLOCK, MOBA_BLOCK), F32),
        ],
        compiler_params=_params("parallel", "parallel", "arbitrary"),
        name="moba_prompt",
    )(q, k, v)


def _means_body(pt_ref, *refs, pages_per_blk, inv_rows):
    page_refs, o_ref = refs[:-1], refs[-1]
    for blk in range(len(page_refs) // pages_per_blk):
        total = jnp.sum(page_refs[blk * pages_per_blk][...], axis=0)
        for p in range(1, pages_per_blk):
            total = total + jnp.sum(page_refs[blk * pages_per_blk + p][...], axis=0)
        o_ref[blk] = total * inv_rows


def _cache_means(cache_k, page_table, *, blocks_per_step):
    _, page, nh, dh = cache_k.shape
    b, n_pages = page_table.shape
    ppb = MOBA_BLOCK // page
    pps = ppb * blocks_per_step
    page_specs = [
        pl.BlockSpec((None, page, nh, dh), lambda bi, s, pt, k=k: (pt[bi, s * pps + k], 0, 0, 0))
        for k in range(pps)
    ]
    return pl.pallas_call(
        functools.partial(_means_body, pages_per_blk=ppb, inv_rows=1.0 / MOBA_BLOCK),
        grid_spec=pltpu.PrefetchScalarGridSpec(
            num_scalar_prefetch=1,
            grid=(b, n_pages // pps),
            in_specs=page_specs,
            out_specs=pl.BlockSpec((None, blocks_per_step, nh, dh), lambda bi, s, pt: (bi, s, 0, 0)),
        ),
        out_shape=jax.ShapeDtypeStruct((b, n_pages // ppb, nh, dh), F32),
        compiler_params=_params("parallel", "arbitrary"),
        name="cache_means",
    )(page_table, *([cache_k] * pps))


def _sgate_body(means_ref, q_ref, o_ref, *, topk):
    nq = q_ref.shape[0]
    nb, nh, _ = means_ref.shape
    o_ref[...] = jnp.zeros(o_ref.shape, jnp.int32)
    means = means_ref[...]
    blk = lax.broadcasted_iota(jnp.int32, (nb, nh), 0)
    for qi in range(nq):
        g = jnp.sum(means * q_ref[qi][None], axis=-1)
        for s in range(topk):
            best = jnp.max(g, axis=0, keepdims=True)
            idx = jnp.min(jnp.where(g == best, blk, nb), axis=0, keepdims=True)
            o_ref[qi, s:s + 1, 0:nh] = idx
            g = jnp.where(blk == idx, -jnp.inf, g)


def _sample_select(means, q4, *, topk):
    b, nb, nh, dh = means.shape
    nq = q4.shape[1]
    return pl.pallas_call(
        functools.partial(_sgate_body, topk=topk),
        grid=(b,),
        in_specs=[
            pl.BlockSpec((None, nb, nh, dh), lambda bi: (bi, 0, 0, 0)),
            pl.BlockSpec((None, nq, nh, dh), lambda bi: (bi, 0, 0, 0)),
        ],
        out_specs=pl.BlockSpec((None, nq, SUBLANES, LANES), lambda bi: (bi, 0, 0, 0)),
        out_shape=jax.ShapeDtypeStruct((b, nq, SUBLANES, LANES), jnp.int32),
        compiler_params=_params("parallel"),
        name="sample_select",
    )(means, q4)


def _sattn_body(pt_ref, idx_ref, q_ref, kn_ref, vn_ref, ck_ref, cv_ref, o_ref,
                kg_ref, vg_ref, sem, *, nq, topk, page, scale):
    b = pl.program_id(0)
    h = pl.program_id(1)
    nh = pl.num_programs(1)
    ppb = MOBA_BLOCK // page
    n_slots = nq * topk

    step = b * nh + h
    buf = lax.rem(step, 2)

    def copies(step_b, step_h, dst, slot, half):
        blk = idx_ref[(step_b * nh + step_h) * n_slots + slot]
        pg = pt_ref[step_b, blk * ppb + half]
        rows = pl.ds(slot * MOBA_BLOCK + half * page, page)
        return (pltpu.make_async_copy(ck_ref.at[pg, :, step_h, :], kg_ref.at[dst, rows, :], sem.at[dst, 0]),
                pltpu.make_async_copy(cv_ref.at[pg, :, step_h, :], vg_ref.at[dst, rows, :], sem.at[dst, 1]))

    def for_all_copies(step_b, step_h, dst, fn):
        for slot in range(n_slots):
            for half in range(ppb):
                for cp in copies(step_b, step_h, dst, slot, half):
                    fn(cp)

    @pl.when(step == 0)
    def _():
        for_all_copies(b, h, buf, lambda cp: cp.start())

    @pl.when(step + 1 < pl.num_programs(0) * nh)
    def _():
        wrap = h + 1 == nh
        for_all_copies(jnp.where(wrap, b + 1, b), jnp.where(wrap, 0, h + 1), 1 - buf,
                       lambda cp: cp.start())

    for_all_copies(b, h, buf, lambda cp: cp.wait())

    rows = q_ref.shape[0]
    qb = q_ref[...].astype(BF16)
    n_keys = n_slots * MOBA_BLOCK
    s_past = lax.dot_general(qb, kg_ref[buf].astype(BF16), _NT, preferred_element_type=F32) * scale
    s_new = lax.dot_general(qb, kn_ref[...].astype(BF16), _NT, preferred_element_type=F32) * scale
    r_past = lax.broadcasted_iota(jnp.int32, (rows, n_keys), 0)
    c_past = lax.broadcasted_iota(jnp.int32, (rows, n_keys), 1)
    lo = r_past * (topk * MOBA_BLOCK)
    m_past = jnp.logical_and(c_past >= lo, c_past < lo + topk * MOBA_BLOCK)
    r_new = lax.broadcasted_iota(jnp.int32, (rows, rows), 0)
    c_new = lax.broadcasted_iota(jnp.int32, (rows, rows), 1)
    m_new = jnp.logical_and(c_new <= r_new, c_new < nq)
    s_past = jnp.where(m_past, s_past, NEG_INF)
    s_new = jnp.where(m_new, s_new, NEG_INF)
    top = jnp.maximum(jnp.max(s_past, axis=1, keepdims=True), jnp.max(s_new, axis=1, keepdims=True))
    p_past = jnp.where(m_past, jnp.exp(s_past - top), 0.0)
    p_new = jnp.where(m_new, jnp.exp(s_new - top), 0.0)
    l = jnp.sum(p_past, axis=1, keepdims=True) + jnp.sum(p_new, axis=1, keepdims=True)
    acc = jnp.dot(p_past.astype(BF16), vg_ref[buf].astype(BF16), preferred_element_type=F32)
    acc = acc + jnp.dot(p_new.astype(BF16), vn_ref[...].astype(BF16), preferred_element_type=F32)
    o_ref[...] = (acc / l).astype(o_ref.dtype)


def _moba_sample(q8, k8, v8, cache_k, cache_v, page_table, sel_idx, *, nq, n_heads):
    b, rows, hd = q8.shape
    dh = hd // n_heads
    page = cache_k.shape[1]
    n_keys = nq * MOBA_TOPK * MOBA_BLOCK
    blk3 = pl.BlockSpec((None, rows, dh), lambda bi, h, pt, idx: (bi, 0, h))
    return pl.pallas_call(
        functools.partial(_sattn_body, nq=nq, topk=MOBA_TOPK, page=page, scale=dh ** -0.5),
        grid_spec=pltpu.PrefetchScalarGridSpec(
            num_scalar_prefetch=2,
            grid=(b, n_heads),
            in_specs=[blk3, blk3, blk3,
                      pl.BlockSpec(memory_space=pl.ANY), pl.BlockSpec(memory_space=pl.ANY)],
            out_specs=blk3,
            scratch_shapes=[
                pltpu.VMEM((2, n_keys, dh), F32),
                pltpu.VMEM((2, n_keys, dh), F32),
                pltpu.SemaphoreType.DMA((2, 2)),
            ],
        ),
        out_shape=jax.ShapeDtypeStruct((b, rows, hd), BF16),
        compiler_params=_params("arbitrary", "arbitrary"),
        name="moba_sample",
    )(page_table, sel_idx, q8, k8, v8, cache_k, cache_v)


def _rope_tables(pos, dh):
    half = dh // 2
    inv = ROPE_THETA ** (-2.0 * jnp.arange(half, dtype=F32) / dh)
    ang = pos.astype(F32)[:, None] * inv[None, :]
    cos, sin = jnp.cos(ang), jnp.sin(ang)
    return jnp.concatenate([cos, cos], axis=1), jnp.concatenate([-sin, sin], axis=1)


def _pad_rows(a, rows):
    return jnp.pad(a, ((0, 0), (0, rows - a.shape[1]), (0, 0)))


def _trunk(x, n_seq, seq_len, pos, conv_prev, past, w, *, tm, tf, tn):
    (lng, lnb, w_ffn_gate, w_ffn_up, w_ffn_down, w_conv_in, w_conv, w_conv_out,
     w_k, w_v, w_q, w_o) = w
    depth = w_ffn_gate.shape[0]
    n_a = w_conv_in.shape[0]
    alpha = float((2 * depth) ** 0.25)
    m, d = x.shape
    n_heads = w_k.shape[1] // LANES
    dh = w_k.shape[1] // n_heads
    cos, sin_signed = _rope_tables(pos, dh)
    if past is not None:
        cos, sin_signed = jnp.tile(cos, (n_seq, 1)), jnp.tile(sin_signed, (n_seq, 1))
    wide = 2 * tn if w_k.shape[1] % (2 * tn) == 0 else tn
    conv_states = []
    for layer in range(depth):
        if layer == n_a:
            k_new, v_new = _proj(x, [(w_k, None), (w_v, None)], (True, False), cos, sin_signed,
                                 tm=tm, tn=wide)
            if past is not None:
                n_blocks = past[2].shape[1] * past[0].shape[1] // MOBA_BLOCK
                means = _cache_means(past[0], past[2], blocks_per_step=2 if n_blocks % 2 == 0 else 1)
        x = yield x, layer, 0, 3 * layer
        if layer < n_a:
            if seq_len < tm:
                prev = conv_prev[layer]
                p1 = jnp.zeros((n_seq, seq_len, d), F32).at[:, 0].set(prev[:, 1])
                p2 = jnp.zeros((n_seq, seq_len, d), F32).at[:, 0].set(prev[:, 0]).at[:, 1].set(prev[:, 1])
                prev_arg = (p1.reshape(m, d), p2.reshape(m, d))
            else:
                prev_arg = jnp.pad(conv_prev[layer], ((0, 0), (SUBLANES - (CONV_W - 1), 0), (0, 0)))
            x, tail = _conv(x, w_conv_in, w_conv, w_conv_out, lng, lnb, layer, 3 * layer + 1,
                            prev_arg, alpha=alpha, tm=tm, tn=tn, seq_len=seq_len)
            if seq_len < tm:
                st = tail.reshape(n_seq, seq_len, d)[:, seq_len - (CONV_W - 1):]
            else:
                tps = seq_len // tm
                st = tail[tps - 1::tps, SUBLANES - (CONV_W - 1):]
            conv_states.append(st)
        else:
            jb = layer - n_a
            (q,) = _proj(x, [(w_q, jb)], (True,), cos, sin_signed, tm=tm, tn=wide)
            if past is None:
                a = _moba_prompt(q.reshape(n_seq, seq_len, -1), k_new.reshape(n_seq, seq_len, -1),
                                 v_new.reshape(n_seq, seq_len, -1), n_heads=n_heads,
                                 hps=4 if n_heads % 4 == 0 else 1)
                a = a.reshape(m, -1)
            else:
                q3 = q.reshape(n_seq, seq_len, -1)
                sel = _sample_select(means, q3.reshape(n_seq, seq_len, n_heads, dh), topk=MOBA_TOPK)
                sel = sel[:, :, :MOBA_TOPK, :n_heads].transpose(0, 3, 1, 2).reshape(-1)
                a8 = _moba_sample(_pad_rows(q3, SUBLANES),
                                  _pad_rows(k_new.reshape(n_seq, seq_len, -1), SUBLANES),
                                  _pad_rows(v_new.reshape(n_seq, seq_len, -1), SUBLANES),
                                  past[0], past[1], past[2], sel, nq=seq_len, n_heads=n_heads)
                a = a8[:, :seq_len].reshape(m, -1)
            x = _oproj(x, a, w_o, lng, lnb, jb, 3 * layer + 1, alpha=alpha, tm=tm, tk=wide)
        x = yield x, layer, 1, 3 * layer + 2
    return x, jnp.stack(conv_states), k_new, v_new


def kernel(x_prompt, x_sample, state_conv, cache_k, cache_v, page_table, ln_g, ln_b, w_ffn_gate, w_ffn_up, w_ffn_down, w_conv_in, w_conv, w_conv_out, w_k, w_v, w_q, w_o):
    bp, tp, d = x_prompt.shape
    bs, ts, _ = x_sample.shape
    n_heads, dh = cache_k.shape[2], cache_k.shape[3]
    n_a = w_conv_in.shape[0]
    past_len = page_table.shape[1] * cache_k.shape[1]
    assert dh == LANES and w_conv.shape[1] == CONV_W
    assert tp % MOBA_BLOCK == 0 and past_len % MOBA_BLOCK == 0 and MOBA_BLOCK % cache_k.shape[1] == 0
    assert CONV_W - 1 <= ts <= SUBLANES and past_len // MOBA_BLOCK >= MOBA_TOPK
    lng = ln_g.reshape(-1, 1, d)
    lnb = ln_b.reshape(-1, 1, d)
    w = (lng, lnb, w_ffn_gate, w_ffn_up, w_ffn_down, w_conv_in, w_conv, w_conv_out, w_k, w_v, w_q, w_o)
    f = w_ffn_gate.shape[-1]
    tf = 256 if f % 256 == 0 else f
    tn = 256 if d % 256 == 0 else d

    conv0 = jnp.zeros((n_a, bp, CONV_W - 1, d), F32)
    tm_p = min(1024, tp)
    pos_s = past_len + jnp.arange(ts, dtype=jnp.int32)
    prompt = _trunk(x_prompt.reshape(bp * tp, d), bp, tp, jnp.arange(tp, dtype=jnp.int32),
                    conv0, None, w, tm=tm_p, tf=tf, tn=tn)
    sample = _trunk(x_sample.reshape(bs * ts, d), bs, ts, pos_s, state_conv,
                    (cache_k, cache_v, page_table), w, tm=bs * ts, tf=tf, tn=tn)
    alpha = float((2 * w_ffn_gate.shape[0]) ** 0.25)
    req_p, req_s = next(prompt), next(sample)
    results = None
    while results is None:
        (xp, layer, sub, ln_idx), xs = req_p, req_s[0]
        yp, ys = _ffn(xp, xs, w_ffn_gate, w_ffn_up, w_ffn_down, lng, lnb, layer, sub, ln_idx,
                      alpha=alpha, tm=tm_p, tf=tf)
        try:
            req_p, req_s = prompt.send(yp), sample.send(ys)
        except StopIteration as done:
            try:
                sample.send(ys)
            except StopIteration as done_s:
                results = done.value, done_s.value
    (y_p, conv_p, k_p, v_p), (y_s, conv_s, k_s, v_s) = results
    return (y_p.reshape(bp, tp, d), y_s.reshape(bs, ts, d), conv_p, conv_s,
            k_p.reshape(bp, tp, n_heads, dh), v_p.reshape(bp, tp, n_heads, dh),
            k_s.reshape(bs, ts, n_heads, dh), v_s.reshape(bs, ts, n_heads, dh))
```

```python
import functools

import jax
import jax.numpy as jnp
from jax import lax
from jax.experimental import pallas as pl
from jax.experimental.pallas import tpu as pltpu

F32 = jnp.float32
BF16 = jnp.bfloat16

MOBA_BLOCK = 256
MOBA_TOPK = 3
CONV_W = 3
ROPE_THETA = 10000.0
LN_EPS = 1e-5
NEG_INF = -1e30
LOG2_E = 1.4426950408889634

SUBLANES = 8
LANES = 128
VMEM_LIMIT_BYTES = 60 * 1024 * 1024

_NT = (((1,), (1,)), ((), ()))


def _params(*sem):
    return pltpu.CompilerParams(dimension_semantics=sem, vmem_limit_bytes=VMEM_LIMIT_BYTES)


def _row_tile_spec(tm, d):
    return pl.BlockSpec((tm, d), lambda i, j: (i, 0), pipeline_mode=pl.Buffered(1))


def _layer_norm(z, g, b):
    mu = jnp.mean(z, axis=-1, keepdims=True)
    d = z - mu
    var = jnp.mean(d * d, axis=-1, keepdims=True)
    return d * lax.rsqrt(var + LN_EPS) * g + b


def _cast_rows(src_ref, dst_ref, rows):
    n = src_ref.shape[0] // rows

    def body(c, carry):
        r = pl.ds(pl.multiple_of(c * rows, rows), rows)
        dst_ref[r, :] = src_ref[r, :].astype(dst_ref.dtype)
        return carry

    lax.fori_loop(0, n, body, 0)


def _start_tile(x_ref, xb_ref, o_ref, j):
    @pl.when(j == 0)
    def _():
        if xb_ref is not None:
            _cast_rows(x_ref, xb_ref, _row_chunk(x_ref.shape[0]))
        o_ref[...] = jnp.zeros(o_ref.shape, o_ref.dtype)


def _post_norm_rows(x_ref, o_ref, g_ref, b_ref, alpha, scale, rows):
    n = o_ref.shape[0] // rows
    g = g_ref[...]
    b = b_ref[...]

    def body(c, carry):
        r = pl.ds(pl.multiple_of(c * rows, rows), rows)
        z = alpha * x_ref[r, :] + scale * o_ref[r, :]
        o_ref[r, :] = _layer_norm(z, g, b)
        return carry

    lax.fori_loop(0, n, body, 0)


def _row_chunk(tm):
    return min(tm, 256)


def _block_means(page_refs, means_ref, pages_per_blk):
    for blk in range(len(page_refs) // pages_per_blk):
        total = jnp.sum(page_refs[blk * pages_per_blk][...], axis=0)
        for p in range(1, pages_per_blk):
            total = total + jnp.sum(page_refs[blk * pages_per_blk + p][...], axis=0)
        means_ref[blk] = total * (1.0 / MOBA_BLOCK)


def _ffn_tile(x_ref, wg_ref, wu_ref, wd_ref, g_ref, b_ref, o_ref, xb_ref, j, side_work,
              *, alpha, n_j):
    _start_tile(x_ref, xb_ref, o_ref, j)
    xb = xb_ref[...]
    gate = jnp.dot(xb, wg_ref[...].astype(BF16), preferred_element_type=F32)
    up = jnp.dot(xb, wu_ref[...].astype(BF16), preferred_element_type=F32)
    side_work()
    h = (jax.nn.silu(gate) * up).astype(BF16)
    o_ref[...] += jnp.dot(h, wd_ref[...].astype(BF16), preferred_element_type=F32)

    @pl.when(j == n_j - 1)
    def _():
        _post_norm_rows(x_ref, o_ref, g_ref, b_ref, alpha, 0.5, _row_chunk(x_ref.shape[0]))


def _ffn_body(*refs, n_p, n_ride, pages_per_blk, **kw):
    if n_ride:
        refs = refs[1:]
    xp_ref, xs_ref, wg_ref, wu_ref, wd_ref, g_ref, b_ref = refs[:7]
    page_refs = refs[7:7 + n_ride]
    op_ref, os_ref = refs[7 + n_ride:9 + n_ride]
    means_ref = refs[9 + n_ride] if n_ride else None
    xbp_ref, xbs_ref = refs[-2:]
    i = pl.program_id(0)
    j = pl.program_id(1)

    def side_work():
        if n_ride:
            _block_means(page_refs, means_ref, pages_per_blk)

    @pl.when(i < n_p)
    def _():
        _ffn_tile(xp_ref, wg_ref, wu_ref, wd_ref, g_ref, b_ref, op_ref, xbp_ref, j, side_work, **kw)

    @pl.when(i == n_p)
    def _():
        _ffn_tile(xs_ref, wg_ref, wu_ref, wd_ref, g_ref, b_ref, os_ref, xbs_ref, j, side_work, **kw)


RIDE_BLOCKS_PER_STEP = 2


def _ffn(xp, xs, w_gate, w_up, w_down, lng, lnb, layer, sub, ln_idx, *, alpha, tm, tf, ride=None):
    mp, d = xp.shape
    ms = xs.shape[0]
    f = w_gate.shape[-1]
    n_p, n_j = mp // tm, f // tf
    prompt_tile = pl.BlockSpec((tm, d), lambda i, j, *_: (jnp.minimum(i, n_p - 1), 0))
    sample_tile = pl.BlockSpec((ms, d), lambda i, j, *_: (0, 0))
    in_specs = [
        prompt_tile,
        sample_tile,
        pl.BlockSpec((None, None, d, tf), lambda i, j, *_: (layer, sub, 0, j)),
        pl.BlockSpec((None, None, d, tf), lambda i, j, *_: (layer, sub, 0, j)),
        pl.BlockSpec((None, None, tf, d), lambda i, j, *_: (layer, sub, j, 0)),
        pl.BlockSpec((None, 1, d), lambda i, j, *_: (ln_idx, 0, 0)),
        pl.BlockSpec((None, 1, d), lambda i, j, *_: (ln_idx, 0, 0)),
    ]
    out_specs = [prompt_tile, sample_tile]
    out_shape = [jax.ShapeDtypeStruct((mp, d), F32), jax.ShapeDtypeStruct((ms, d), F32)]
    operands = [xp, xs, w_gate, w_up, w_down, lng, lnb]
    n_ride = ppb = 0
    prefetch = []
    if ride is not None:
        cache_k, page_table, first_block = ride
        _, page, nh, dh = cache_k.shape
        n_pages = page_table.shape[1]
        ppb = MOBA_BLOCK // page
        bps = RIDE_BLOCKS_PER_STEP
        n_here = min((n_p + 1) * n_j * bps, page_table.size // ppb - first_block)
        assert n_here > 0 and n_here % bps == 0
        last_step = n_here // bps - 1
        n_ride = bps * ppb

        def step_of(i, j):
            return jnp.minimum(i * n_j + j, last_step)

        def page_spec(k):
            def index(i, j, pt):
                pg = (first_block + step_of(i, j) * bps) * ppb + k
                return (pt[pg // n_pages, pg % n_pages], 0, 0, 0)
            return pl.BlockSpec((None, page, nh, dh), index)

        in_specs += [page_spec(k) for k in range(n_ride)]
        operands += [cache_k] * n_ride
        out_specs.append(pl.BlockSpec((bps, nh, dh), lambda i, j, pt: (step_of(i, j), 0, 0)))
        out_shape.append(jax.ShapeDtypeStruct((n_here, nh, dh), F32))
        prefetch = [page_table]
    return pl.pallas_call(
        functools.partial(_ffn_body, alpha=alpha, n_j=n_j, n_p=n_p, n_ride=n_ride, pages_per_blk=ppb),
        grid_spec=pltpu.PrefetchScalarGridSpec(
            num_scalar_prefetch=len(prefetch),
            grid=(n_p + 1, n_j),
            in_specs=in_specs,
            out_specs=out_specs,
            scratch_shapes=[pltpu.VMEM((tm, d), BF16), pltpu.VMEM((ms, d), BF16)],
        ),
        out_shape=out_shape,
        compiler_params=_params("arbitrary", "arbitrary"),
        name="ffn",
    )(*prefetch, *operands)


def _conv_body(*refs, alpha, n_j, tiles_per_seq, seq_len, multi_seq):
    if multi_seq:
        (x_ref, wb_ref, wc_ref, wh_ref, wk_ref, wo_ref, g_ref, b_ref, p1_ref, p2_ref,
         o_ref, tail_ref, xb_ref, halo_ref) = refs
    else:
        (x_ref, wb_ref, wc_ref, wh_ref, wk_ref, wo_ref, g_ref, b_ref, prev_ref,
         o_ref, tail_ref, xb_ref, halo_ref) = refs
    i = pl.program_id(0)
    j = pl.program_id(1)
    tm = x_ref.shape[0]
    tail_rows = tail_ref.shape[0]

    _start_tile(x_ref, xb_ref, o_ref, j)
    if not multi_seq:
        @pl.when((i % tiles_per_seq) == 0)
        def _():
            halo_ref[j] = prev_ref[...]

    xb = xb_ref[...]
    gate_b = jnp.dot(xb, wb_ref[...].astype(BF16), preferred_element_type=F32)
    gate_c = jnp.dot(xb, wc_ref[...].astype(BF16), preferred_element_type=F32)
    whb = wh_ref[...].astype(BF16)
    half = tm // 2 if tm % (4 * SUBLANES) == 0 else tm
    hh = jnp.concatenate([jnp.dot(xb[r:r + half], whb, preferred_element_type=F32)
                          for r in range(0, tm, half)], axis=0)
    u = gate_c * hh
    s1 = pltpu.roll(u, 1, axis=0)
    s2 = pltpu.roll(u, 2, axis=0)
    if multi_seq:
        t = lax.broadcasted_iota(jnp.int32, u.shape, 0) % seq_len
        s1 = jnp.where(t >= 1, s1, p1_ref[...])
        s2 = jnp.where(t >= 2, s2, p2_ref[...])
    else:
        halo = halo_ref[j]
        row = lax.broadcasted_iota(jnp.int32, halo.shape, 0)
        before1 = jnp.broadcast_to(halo[SUBLANES - 1:SUBLANES, :], halo.shape)
        before2 = jnp.broadcast_to(halo[SUBLANES - 2:SUBLANES - 1, :], halo.shape)
        top1 = jnp.where(row == 0, before1, s1[0:SUBLANES, :])
        top2 = jnp.where(row == 0, before2, jnp.where(row == 1, before1, s2[0:SUBLANES, :]))
        s1 = jnp.concatenate([top1, s1[SUBLANES:, :]], axis=0)
        s2 = jnp.concatenate([top2, s2[SUBLANES:, :]], axis=0)
    wk = wk_ref[...]
    y = wk[0:1, :] * s2
    y = y + wk[1:2, :] * s1
    y = y + wk[2:3, :] * u
    yb = (gate_b * y).astype(BF16)
    o_ref[...] += jnp.dot(yb, wo_ref[...].astype(BF16), preferred_element_type=F32)

    halo_ref[j] = u[tm - SUBLANES:, :]
    tail_ref[...] = u[tm - tail_rows:, :]

    @pl.when(j == n_j - 1)
    def _():
        _post_norm_rows(x_ref, o_ref, g_ref, b_ref, alpha, 1.0, _row_chunk(tm))


def _conv(x, w_in, w_conv, w_out, lng, lnb, layer, ln_idx, prev, *, alpha, tm, tn, seq_len):
    m, d = x.shape
    n_i, n_j = m // tm, d // tn
    multi_seq = seq_len < tm
    tiles_per_seq = max(seq_len // tm, 1)
    tail_rows = tm if multi_seq else SUBLANES
    in_specs = [
        _row_tile_spec(tm, d),
        pl.BlockSpec((None, d, tn), lambda i, j: (layer, 0, j)),
        pl.BlockSpec((None, d, tn), lambda i, j: (layer, 0, n_j + j)),
        pl.BlockSpec((None, d, tn), lambda i, j: (layer, 0, 2 * n_j + j)),
        pl.BlockSpec((None, CONV_W, tn), lambda i, j: (layer, 0, j)),
        pl.BlockSpec((None, tn, d), lambda i, j: (layer, j, 0)),
        pl.BlockSpec((None, 1, d), lambda i, j: (ln_idx, 0, 0)),
        pl.BlockSpec((None, 1, d), lambda i, j: (ln_idx, 0, 0)),
    ]
    if multi_seq:
        in_specs += [pl.BlockSpec((tm, tn), lambda i, j: (i, j))] * 2
        extra = tuple(prev)
    else:
        in_specs += [pl.BlockSpec((None, SUBLANES, tn), lambda i, j: (i // tiles_per_seq, 0, j))]
        extra = (prev,)
    return pl.pallas_call(
        functools.partial(_conv_body, alpha=alpha, n_j=n_j, tiles_per_seq=tiles_per_seq,
                          seq_len=seq_len, multi_seq=multi_seq),
        grid=(n_i, n_j),
        in_specs=in_specs,
        out_specs=[
            pl.BlockSpec((tm, d), lambda i, j: (i, 0)),
            pl.BlockSpec((None, tail_rows, tn), lambda i, j: (i, 0, j)),
        ],
        out_shape=[
            jax.ShapeDtypeStruct((m, d), F32),
            jax.ShapeDtypeStruct((n_i, tail_rows, d), F32),
        ],
        scratch_shapes=[
            pltpu.VMEM((tm, d), BF16),
            pltpu.VMEM((n_j, SUBLANES, tn), F32),
        ],
        compiler_params=_params("arbitrary", "arbitrary"),
        name="conv",
    )(x, w_in, w_in, w_in, w_conv, w_out, lng, lnb, *extra)


def _rope_cols(y, cos, sin_signed):
    heads = []
    for c in range(y.shape[1] // LANES):
        yh = y[:, c * LANES:(c + 1) * LANES]
        heads.append(yh * cos + pltpu.roll(yh, LANES // 2, axis=1) * sin_signed)
    return heads[0] if len(heads) == 1 else jnp.concatenate(heads, axis=1)


def _proj_body(*refs, rope):
    n_w = len(rope)
    x_ref = refs[0]
    w_refs = refs[1:1 + n_w]
    cos_ref, sin_ref = refs[1 + n_w:3 + n_w]
    o_refs = refs[3 + n_w:3 + 2 * n_w]
    xb_ref = refs[3 + 2 * n_w]
    j = pl.program_id(1)

    @pl.when(j == 0)
    def _():
        _cast_rows(x_ref, xb_ref, _row_chunk(x_ref.shape[0]))

    xb = xb_ref[...]
    for w_ref, o_ref, use_rope in zip(w_refs, o_refs, rope):
        y = jnp.dot(xb, w_ref[...].astype(BF16), preferred_element_type=F32)
        if use_rope:
            y = _rope_cols(y, cos_ref[...], sin_ref[...])
        o_ref[...] = y


def _proj(x, weights, rope, cos, sin_signed, *, tm, tn):
    m, d = x.shape
    n = weights[0][0].shape[-1]
    n_i, n_j = m // tm, n // tn
    pos_tiles = cos.shape[0] // tm
    w_specs = []
    for w, lead in weights:
        if lead is None:
            w_specs.append(pl.BlockSpec((d, tn), lambda i, j: (0, j)))
        else:
            w_specs.append(pl.BlockSpec((None, d, tn), lambda i, j, lead=lead: (lead, 0, j)))
    pos_spec = pl.BlockSpec((tm, LANES), lambda i, j: (i % pos_tiles, 0))
    return pl.pallas_call(
        functools.partial(_proj_body, rope=tuple(rope)),
        grid=(n_i, n_j),
        in_specs=[pl.BlockSpec((tm, d), lambda i, j: (i, 0))] + w_specs + [pos_spec, pos_spec],
        out_specs=[pl.BlockSpec((tm, tn), lambda i, j: (i, j))] * len(weights),
        out_shape=[jax.ShapeDtypeStruct((m, n), F32)] * len(weights),
        scratch_shapes=[pltpu.VMEM((tm, d), BF16)],
        compiler_params=_params("parallel", "arbitrary"),
        name="proj",
    )(x, *[w for w, _ in weights], cos, sin_signed)


def _oproj_body(x_ref, a_ref, w_ref, g_ref, b_ref, o_ref, *, alpha, n_j):
    j = pl.program_id(1)
    _start_tile(x_ref, None, o_ref, j)
    o_ref[...] += jnp.dot(a_ref[...], w_ref[...].astype(BF16), preferred_element_type=F32)

    @pl.when(j == n_j - 1)
    def _():
        _post_norm_rows(x_ref, o_ref, g_ref, b_ref, alpha, 1.0, _row_chunk(x_ref.shape[0]))


def _oproj(x, a, w_o, lng, lnb, layer, ln_idx, *, alpha, tm, tk):
    m, d = x.shape
    hd = a.shape[1]
    n_i, n_j = m // tm, hd // tk
    return pl.pallas_call(
        functools.partial(_oproj_body, alpha=alpha, n_j=n_j),
        grid=(n_i, n_j),
        in_specs=[
            pl.BlockSpec((tm, d), lambda i, j: (i, 0)),
            pl.BlockSpec((tm, tk), lambda i, j: (i, j)),
            pl.BlockSpec((None, tk, d), lambda i, j: (layer, j, 0)),
            pl.BlockSpec((None, 1, d), lambda i, j: (ln_idx, 0, 0)),
            pl.BlockSpec((None, 1, d), lambda i, j: (ln_idx, 0, 0)),
        ],
        out_specs=pl.BlockSpec((tm, d), lambda i, j: (i, 0)),
        out_shape=jax.ShapeDtypeStruct((m, d), F32),
        compiler_params=_params("parallel", "arbitrary"),
        name="oproj",
    )(x, a, w_o, lng, lnb)


def _select_blocks(gate_t, n_valid, topk):
    nb = gate_t.shape[0]
    blk = lax.broadcasted_iota(jnp.int32, gate_t.shape, 0)
    g = jnp.where(blk < n_valid, gate_t, NEG_INF)
    sel = jnp.zeros(gate_t.shape, jnp.bool_)
    for s in range(topk):
        best = jnp.max(g, axis=0, keepdims=True)
        idx = jnp.min(jnp.where(g == best, blk, nb), axis=0, keepdims=True)
        hit = blk == idx
        sel = jnp.logical_or(sel, jnp.logical_and(hit, s < n_valid))
        g = jnp.where(hit, -jnp.inf, g)
    return sel


def _attn_body(q_ref, k_ref, v_ref, o_ref, means_ref, kb_ref, vt_ref, bias_ref, s_ref, p_ref,
               *, scale, hps):
    i = pl.program_id(2)
    blk = MOBA_BLOCK
    nb = k_ref.shape[0] // blk
    dh = LANES

    @pl.when(i == 0)
    def _():
        for hh in range(hps):
            cols = slice(hh * dh, (hh + 1) * dh)
            for n in range(nb):
                rows = slice(n * blk, (n + 1) * blk)
                kf = k_ref[rows, cols]
                means_ref[hh, n:n + 1, :] = jnp.sum(kf, axis=0, keepdims=True) * (1.0 / blk)
                kb_ref[hh, n] = kf.astype(BF16)
                vt_ref[hh, n] = v_ref[rows, cols].T.astype(BF16)

    key = lax.broadcasted_iota(jnp.int32, (blk, blk), 0)
    qry = lax.broadcasted_iota(jnp.int32, (blk, blk), 1)
    qs = [q_ref[:, hh * dh:(hh + 1) * dh] for hh in range(hps)]
    qbs = [(q * (scale * LOG2_E)).astype(BF16) for q in qs]

    def logits(n):
        return [lax.dot_general(kb_ref[hh, n], qbs[hh], _NT, preferred_element_type=F32)
                for hh in range(hps)]

    own, first_past = logits(i), logits(0)
    for hh in range(hps):
        gate_t = lax.dot_general(means_ref[hh], qs[hh], _NT, precision=lax.Precision.HIGHEST,
                                 preferred_element_type=F32)
        bias_ref[hh] = jnp.where(_select_blocks(gate_t, i, MOBA_TOPK), 0.0, NEG_INF)
    carry = []
    for hh in range(hps):
        s = jnp.where(key <= qry, own[hh], NEG_INF)
        m = jnp.max(s, axis=0, keepdims=True)
        p = jnp.exp2(s - m)
        p_ref[hh] = p.astype(BF16)
        s_ref[hh] = first_past[hh]
        carry += [m, jnp.sum(p, axis=0, keepdims=True), jnp.zeros((dh, blk), F32),
                  jnp.ones((1, blk), F32)]

    def pv_products(v_blk):
        return [jnp.dot(vt_ref[hh, v_blk], p_ref[hh], preferred_element_type=F32)
                for hh in range(hps)]

    def past(n, carry):
        ss_next = logits(jnp.minimum(n + 1, nb - 1))
        pvs = pv_products(jnp.where(n == 0, i, n - 1))
        out, ps = [], []
        for hh in range(hps):
            m, l, acc, corr_prev = carry[4 * hh:4 * hh + 4]
            s = s_ref[hh]
            bias = bias_ref[hh, pl.ds(n, 1), :]
            m_new = jnp.maximum(m, jnp.max(s, axis=0, keepdims=True) + bias)
            p = jnp.exp2(s + (bias - m_new))
            corr = jnp.exp2(m - m_new)
            ps.append(p.astype(BF16))
            out += [m_new, l * corr + jnp.sum(p, axis=0, keepdims=True),
                    acc * corr_prev + pvs[hh], corr]
        for hh in range(hps):
            p_ref[hh] = ps[hh]
            s_ref[hh] = ss_next[hh]
        return tuple(out)

    carry = lax.fori_loop(0, i, past, tuple(carry))
    pvs = pv_products(jnp.where(i == 0, i, i - 1))
    for hh in range(hps):
        _, l, acc, corr_prev = carry[4 * hh:4 * hh + 4]
        acc = acc * corr_prev + pvs[hh]
        o_ref[:, hh * dh:(hh + 1) * dh] = (acc / l).T.astype(o_ref.dtype)


def _moba_prompt(q, k, v, *, n_heads, hps):
    b, t, hd = q.shape
    dh = hd // n_heads
    nb = t // MOBA_BLOCK
    w = hps * dh
    return pl.pallas_call(
        functools.partial(_attn_body, scale=dh ** -0.5, hps=hps),
        grid=(b, n_heads // hps, nb),
        in_specs=[
            pl.BlockSpec((None, MOBA_BLOCK, w), lambda bi, h, i: (bi, i, h)),
            pl.BlockSpec((None, t, w), lambda bi, h, i: (bi, 0, h)),
            pl.BlockSpec((None, t, w), lambda bi, h, i: (bi, 0, h)),
        ],
        out_specs=pl.BlockSpec((None, MOBA_BLOCK, w), lambda bi, h, i: (bi, i, h)),
        out_shape=jax.ShapeDtypeStruct((b, t, hd), BF16),
        scratch_shapes=[
            pltpu.VMEM((hps, nb, dh), F32),
            pltpu.VMEM((hps, nb, MOBA_BLOCK, dh), BF16),
            pltpu.VMEM((hps, nb, dh, MOBA_BLOCK), BF16),
            pltpu.VMEM((hps, nb, MOBA_BLOCK), F32),
            pltpu.VMEM((hps, MOBA_BLOCK, MOBA_BLOCK), F32),
            pltpu.VMEM((hps, MOBA_BLOCK, MOBA_BLOCK), BF16),
        ],
        compiler_params=_params("parallel", "parallel", "arbitrary"),
        name="moba_prompt",
    )(q, k, v)


def _means_body(pt_ref, *refs, pages_per_blk):
    _block_means(refs[:-1], refs[-1], pages_per_blk)


def _cache_means(cache_k, page_table, *, blocks_per_step):
    _, page, nh, dh = cache_k.shape
    b, n_pages = page_table.shape
    ppb = MOBA_BLOCK // page
    pps = ppb * blocks_per_step
    page_specs = [
        pl.BlockSpec((None, page, nh, dh), lambda bi, s, pt, k=k: (pt[bi, s * pps + k], 0, 0, 0))
        for k in range(pps)
    ]
    return pl.pallas_call(
        functools.partial(_means_body, pages_per_blk=ppb),
        grid_spec=pltpu.PrefetchScalarGridSpec(
            num_scalar_prefetch=1,
            grid=(b, n_pages // pps),
            in_specs=page_specs,
            out_specs=pl.BlockSpec((None, blocks_per_step, nh, dh), lambda bi, s, pt: (bi, s, 0, 0)),
        ),
        out_shape=jax.ShapeDtypeStruct((b, n_pages // ppb, nh, dh), F32),
        compiler_params=_params("parallel", "arbitrary"),
        name="cache_means",
    )(page_table, *([cache_k] * pps))


def _sgate_body(means_ref, q_ref, o_ref, *, topk):
    nq = q_ref.shape[0]
    nb, nh, _ = means_ref.shape
    o_ref[...] = jnp.zeros(o_ref.shape, jnp.int32)
    means = means_ref[...]
    blk = lax.broadcasted_iota(jnp.int32, (nb, nh), 0)
    for qi in range(nq):
        g = jnp.sum(means * q_ref[qi][None], axis=-1)
        for s in range(topk):
            best = jnp.max(g, axis=0, keepdims=True)
            idx = jnp.min(jnp.where(g == best, blk, nb), axis=0, keepdims=True)
            o_ref[qi, s:s + 1, 0:nh] = idx
            g = jnp.where(blk == idx, -jnp.inf, g)


def _sample_select(means, q4, *, topk):
    b, nb, nh, dh = means.shape
    nq = q4.shape[1]
    return pl.pallas_call(
        functools.partial(_sgate_body, topk=topk),
        grid=(b,),
        in_specs=[
            pl.BlockSpec((None, nb, nh, dh), lambda bi: (bi, 0, 0, 0)),
            pl.BlockSpec((None, nq, nh, dh), lambda bi: (bi, 0, 0, 0)),
        ],
        out_specs=pl.BlockSpec((None, nq, SUBLANES, LANES), lambda bi: (bi, 0, 0, 0)),
        out_shape=jax.ShapeDtypeStruct((b, nq, SUBLANES, LANES), jnp.int32),
        compiler_params=_params("parallel"),
        name="sample_select",
    )(means, q4)


def _sattn_body(pt_ref, idx_ref, q_ref, kn_ref, vn_ref, ck_ref, cv_ref, o_ref,
                kg_ref, vg_ref, sem, *, nq, topk, page, scale):
    b = pl.program_id(0)
    h = pl.program_id(1)
    nh = pl.num_programs(1)
    ppb = MOBA_BLOCK // page
    n_slots = nq * topk

    step = b * nh + h
    buf = lax.rem(step, 2)

    def copies(step_b, step_h, dst, slot, half):
        blk = idx_ref[(step_b * nh + step_h) * n_slots + slot]
        pg = pt_ref[step_b, blk * ppb + half]
        rows = pl.ds(slot * MOBA_BLOCK + half * page, page)
        return (pltpu.make_async_copy(ck_ref.at[pg, :, step_h, :], kg_ref.at[dst, rows, :], sem.at[dst, 0]),
                pltpu.make_async_copy(cv_ref.at[pg, :, step_h, :], vg_ref.at[dst, rows, :], sem.at[dst, 1]))

    def for_all_copies(step_b, step_h, dst, fn):
        for slot in range(n_slots):
            for half in range(ppb):
                for cp in copies(step_b, step_h, dst, slot, half):
                    fn(cp)

    @pl.when(step == 0)
    def _():
        for_all_copies(b, h, buf, lambda cp: cp.start())

    @pl.when(step + 1 < pl.num_programs(0) * nh)
    def _():
        wrap = h + 1 == nh
        for_all_copies(jnp.where(wrap, b + 1, b), jnp.where(wrap, 0, h + 1), 1 - buf,
                       lambda cp: cp.start())

    for_all_copies(b, h, buf, lambda cp: cp.wait())

    rows = q_ref.shape[0]
    qb = q_ref[...].astype(BF16)
    n_keys = n_slots * MOBA_BLOCK
    s_past = lax.dot_general(qb, kg_ref[buf].astype(BF16), _NT, preferred_element_type=F32) * scale
    s_new = lax.dot_general(qb, kn_ref[...].astype(BF16), _NT, preferred_element_type=F32) * scale
    r_past = lax.broadcasted_iota(jnp.int32, (rows, n_keys), 0)
    c_past = lax.broadcasted_iota(jnp.int32, (rows, n_keys), 1)
    lo = r_past * (topk * MOBA_BLOCK)
    m_past = jnp.logical_and(c_past >= lo, c_past < lo + topk * MOBA_BLOCK)
    r_new = lax.broadcasted_iota(jnp.int32, (rows, rows), 0)
    c_new = lax.broadcasted_iota(jnp.int32, (rows, rows), 1)
    m_new = jnp.logical_and(c_new <= r_new, c_new < nq)
    s_past = jnp.where(m_past, s_past, NEG_INF)
    s_new = jnp.where(m_new, s_new, NEG_INF)
    top = jnp.maximum(jnp.max(s_past, axis=1, keepdims=True), jnp.max(s_new, axis=1, keepdims=True))
    p_past = jnp.where(m_past, jnp.exp(s_past - top), 0.0)
    p_new = jnp.where(m_new, jnp.exp(s_new - top), 0.0)
    l = jnp.sum(p_past, axis=1, keepdims=True) + jnp.sum(p_new, axis=1, keepdims=True)
    acc = jnp.dot(p_past.astype(BF16), vg_ref[buf].astype(BF16), preferred_element_type=F32)
    acc = acc + jnp.dot(p_new.astype(BF16), vn_ref[...].astype(BF16), preferred_element_type=F32)
    o_ref[...] = (acc / l).astype(o_ref.dtype)


def _moba_sample(q8, k8, v8, cache_k, cache_v, page_table, sel_idx, *, nq, n_heads):
    b, rows, hd = q8.shape
    dh = hd // n_heads
    page = cache_k.shape[1]
    n_keys = nq * MOBA_TOPK * MOBA_BLOCK
    blk3 = pl.BlockSpec((None, rows, dh), lambda bi, h, pt, idx: (bi, 0, h))
    return pl.pallas_call(
        functools.partial(_sattn_body, nq=nq, topk=MOBA_TOPK, page=page, scale=dh ** -0.5),
        grid_spec=pltpu.PrefetchScalarGridSpec(
            num_scalar_prefetch=2,
            grid=(b, n_heads),
            in_specs=[blk3, blk3, blk3,
                      pl.BlockSpec(memory_space=pl.ANY), pl.BlockSpec(memory_space=pl.ANY)],
            out_specs=blk3,
            scratch_shapes=[
                pltpu.VMEM((2, n_keys, dh), F32),
                pltpu.VMEM((2, n_keys, dh), F32),
                pltpu.SemaphoreType.DMA((2, 2)),
            ],
        ),
        out_shape=jax.ShapeDtypeStruct((b, rows, hd), BF16),
        compiler_params=_params("arbitrary", "arbitrary"),
        name="moba_sample",
    )(page_table, sel_idx, q8, k8, v8, cache_k, cache_v)


def _rope_tables(pos, dh):
    half = dh // 2
    inv = ROPE_THETA ** (-2.0 * jnp.arange(half, dtype=F32) / dh)
    ang = pos.astype(F32)[:, None] * inv[None, :]
    cos, sin = jnp.cos(ang), jnp.sin(ang)
    return jnp.concatenate([cos, cos], axis=1), jnp.concatenate([-sin, sin], axis=1)


def _pad_rows(a, rows):
    return jnp.pad(a, ((0, 0), (0, rows - a.shape[1]), (0, 0)))


def _trunk(x, n_seq, seq_len, pos, conv_prev, past, w, *, tm, tf, tn):
    (lng, lnb, w_ffn_gate, w_ffn_up, w_ffn_down, w_conv_in, w_conv, w_conv_out,
     w_k, w_v, w_q, w_o) = w
    depth = w_ffn_gate.shape[0]
    n_a = w_conv_in.shape[0]
    alpha = float((2 * depth) ** 0.25)
    m, d = x.shape
    n_heads = w_k.shape[1] // LANES
    dh = w_k.shape[1] // n_heads
    cos, sin_signed = _rope_tables(pos, dh)
    if past is not None:
        cos, sin_signed = jnp.tile(cos, (n_seq, 1)), jnp.tile(sin_signed, (n_seq, 1))
    wide = 2 * tn if w_k.shape[1] % (2 * tn) == 0 else tn
    conv_states = []
    for layer in range(depth):
        if layer == n_a:
            k_new, v_new = _proj(x, [(w_k, None), (w_v, None)], (True, False), cos, sin_signed,
                                 tm=tm, tn=wide)
            if past is not None:
                means = past[3]()
        x = yield x, layer, 0, 3 * layer
        if layer < n_a:
            if seq_len < tm:
                prev = conv_prev[layer]
                p1 = jnp.zeros((n_seq, seq_len, d), F32).at[:, 0].set(prev[:, 1])
                p2 = jnp.zeros((n_seq, seq_len, d), F32).at[:, 0].set(prev[:, 0]).at[:, 1].set(prev[:, 1])
                prev_arg = (p1.reshape(m, d), p2.reshape(m, d))
            else:
                prev_arg = jnp.pad(conv_prev[layer], ((0, 0), (SUBLANES - (CONV_W - 1), 0), (0, 0)))
            x, tail = _conv(x, w_conv_in, w_conv, w_conv_out, lng, lnb, layer, 3 * layer + 1,
                            prev_arg, alpha=alpha, tm=tm, tn=tn, seq_len=seq_len)
            if seq_len < tm:
                st = tail.reshape(n_seq, seq_len, d)[:, seq_len - (CONV_W - 1):]
            else:
                tps = seq_len // tm
                st = tail[tps - 1::tps, SUBLANES - (CONV_W - 1):]
            conv_states.append(st)
        else:
            jb = layer - n_a
            (q,) = _proj(x, [(w_q, jb)], (True,), cos, sin_signed, tm=tm, tn=wide)
            if past is None:
                a = _moba_prompt(q.reshape(n_seq, seq_len, -1), k_new.reshape(n_seq, seq_len, -1),
                                 v_new.reshape(n_seq, seq_len, -1), n_heads=n_heads,
                                 hps=4 if n_heads % 4 == 0 else 1)
                a = a.reshape(m, -1)
            else:
                q3 = q.reshape(n_seq, seq_len, -1)
                sel = _sample_select(means, q3.reshape(n_seq, seq_len, n_heads, dh), topk=MOBA_TOPK)
                sel = sel[:, :, :MOBA_TOPK, :n_heads].transpose(0, 3, 1, 2).reshape(-1)
                a8 = _moba_sample(_pad_rows(q3, SUBLANES),
                                  _pad_rows(k_new.reshape(n_seq, seq_len, -1), SUBLANES),
                                  _pad_rows(v_new.reshape(n_seq, seq_len, -1), SUBLANES),
                                  past[0], past[1], past[2], sel, nq=seq_len, n_heads=n_heads)
                a = a8[:, :seq_len].reshape(m, -1)
            x = _oproj(x, a, w_o, lng, lnb, jb, 3 * layer + 1, alpha=alpha, tm=tm, tk=wide)
        x = yield x, layer, 1, 3 * layer + 2
    return x, jnp.stack(conv_states), k_new, v_new


def kernel(x_prompt, x_sample, state_conv, cache_k, cache_v, page_table, ln_g, ln_b, w_ffn_gate, w_ffn_up, w_ffn_down, w_conv_in, w_conv, w_conv_out, w_k, w_v, w_q, w_o):
    bp, tp, d = x_prompt.shape
    bs, ts, _ = x_sample.shape
    n_heads, dh = cache_k.shape[2], cache_k.shape[3]
    n_a = w_conv_in.shape[0]
    past_len = page_table.shape[1] * cache_k.shape[1]
    assert dh == LANES and w_conv.shape[1] == CONV_W
    assert tp % MOBA_BLOCK == 0 and past_len % MOBA_BLOCK == 0 and MOBA_BLOCK % cache_k.shape[1] == 0
    assert CONV_W - 1 <= ts <= SUBLANES and past_len // MOBA_BLOCK >= MOBA_TOPK
    lng = ln_g.reshape(-1, 1, d)
    lnb = ln_b.reshape(-1, 1, d)
    w = (lng, lnb, w_ffn_gate, w_ffn_up, w_ffn_down, w_conv_in, w_conv, w_conv_out, w_k, w_v, w_q, w_o)
    f = w_ffn_gate.shape[-1]
    tf = 256 if f % 256 == 0 else f
    tn = 256 if d % 256 == 0 else d

    conv0 = jnp.zeros((n_a, bp, CONV_W - 1, d), F32)
    tm_p = min(1024, tp)
    pos_s = past_len + jnp.arange(ts, dtype=jnp.int32)
    prompt = _trunk(x_prompt.reshape(bp * tp, d), bp, tp, jnp.arange(tp, dtype=jnp.int32),
                    conv0, None, w, tm=tm_p, tf=tf, tn=tn)
    n_blocks = past_len // MOBA_BLOCK
    mean_parts = []

    def block_means():
        if sum(part.shape[0] for part in mean_parts) == bs * n_blocks:
            return jnp.concatenate(mean_parts, axis=0).reshape(bs, n_blocks, n_heads, dh)
        return _cache_means(cache_k, page_table, blocks_per_step=2 if n_blocks % 2 == 0 else 1)

    sample = _trunk(x_sample.reshape(bs * ts, d), bs, ts, pos_s, state_conv,
                    (cache_k, cache_v, page_table, block_means), w, tm=bs * ts, tf=tf, tn=tn)
    alpha = float((2 * w_ffn_gate.shape[0]) ** 0.25)
    req_p, req_s = next(prompt), next(sample)
    results = None
    while results is None:
        (xp, layer, sub, ln_idx), xs = req_p, req_s[0]
        first_block = sum(part.shape[0] for part in mean_parts)
        ride = (cache_k, page_table, first_block) if first_block < bs * n_blocks else None
        yp, ys, *part = _ffn(xp, xs, w_ffn_gate, w_ffn_up, w_ffn_down, lng, lnb, layer, sub, ln_idx,
                             alpha=alpha, tm=tm_p, tf=tf, ride=ride)
        mean_parts += part
        try:
            req_p, req_s = prompt.send(yp), sample.send(ys)
        except StopIteration as done:
            try:
                sample.send(ys)
            except StopIteration as done_s:
                results = done.value, done_s.value
    (y_p, conv_p, k_p, v_p), (y_s, conv_s, k_s, v_s) = results
    return (y_p.reshape(bp, tp, d), y_s.reshape(bs, ts, d), conv_p, conv_s,
            k_p.reshape(bp, tp, n_heads, dh), v_p.reshape(bp, tp, n_heads, dh),
            k_s.reshape(bs, ts, n_heads, dh), v_s.reshape(bs, ts, n_heads, dh))
```

```python
import functools

import jax
import jax.numpy as jnp
from jax import lax
from jax.experimental import pallas as pl
from jax.experimental.pallas import tpu as pltpu

F32 = jnp.float32
BF16 = jnp.bfloat16

MOBA_BLOCK = 256
MOBA_TOPK = 3
CONV_W = 3
ROPE_THETA = 10000.0
LN_EPS = 1e-5
NEG_INF = -1e30
LOG2_E = 1.4426950408889634

SUBLANES = 8
LANES = 128
VMEM_LIMIT_BYTES = 60 * 1024 * 1024

_NT = (((1,), (1,)), ((), ()))


def _params(*sem):
    return pltpu.CompilerParams(dimension_semantics=sem, vmem_limit_bytes=VMEM_LIMIT_BYTES)


def _layer_norm(z, g, b):
    mu = jnp.mean(z, axis=-1, keepdims=True)
    d = z - mu
    var = jnp.mean(d * d, axis=-1, keepdims=True)
    return d * lax.rsqrt(var + LN_EPS) * g + b


def _cast_rows(src_ref, dst_ref, rows):
    n = src_ref.shape[0] // rows

    def body(c, carry):
        r = pl.ds(pl.multiple_of(c * rows, rows), rows)
        dst_ref[r, :] = src_ref[r, :].astype(dst_ref.dtype)
        return carry

    lax.fori_loop(0, n, body, 0)


def _start_tile(x_ref, xb_ref, o_ref, j):
    @pl.when(j == 0)
    def _():
        if xb_ref is not None:
            _cast_rows(x_ref, xb_ref, _row_chunk(x_ref.shape[0]))
        o_ref[...] = jnp.zeros(o_ref.shape, o_ref.dtype)


def _post_norm_rows(x_ref, o_ref, g_ref, b_ref, alpha, scale, rows):
    n = o_ref.shape[0] // rows
    g = g_ref[...]
    b = b_ref[...]

    def body(c, carry):
        r = pl.ds(pl.multiple_of(c * rows, rows), rows)
        z = alpha * x_ref[r, :] + scale * o_ref[r, :]
        o_ref[r, :] = _layer_norm(z, g, b)
        return carry

    lax.fori_loop(0, n, body, 0)


def _row_chunk(tm):
    return min(tm, 256)


def _block_means(page_refs, means_ref, pages_per_blk):
    for blk in range(len(page_refs) // pages_per_blk):
        total = jnp.sum(page_refs[blk * pages_per_blk][...], axis=0)
        for p in range(1, pages_per_blk):
            total = total + jnp.sum(page_refs[blk * pages_per_blk + p][...], axis=0)
        means_ref[blk] = total * (1.0 / MOBA_BLOCK)


def _ffn_tile(x_ref, wg_ref, wu_ref, wd_ref, g_ref, b_ref, o_ref, xb_ref, j, side_work,
              *, alpha, n_j):
    _start_tile(x_ref, xb_ref, o_ref, j)
    xb = xb_ref[...]
    gate = jnp.dot(xb, wg_ref[...].astype(BF16), preferred_element_type=F32)
    up = jnp.dot(xb, wu_ref[...].astype(BF16), preferred_element_type=F32)
    side_work()
    h = (jax.nn.silu(gate) * up).astype(BF16)
    o_ref[...] += jnp.dot(h, wd_ref[...].astype(BF16), preferred_element_type=F32)

    @pl.when(j == n_j - 1)
    def _():
        _post_norm_rows(x_ref, o_ref, g_ref, b_ref, alpha, 0.5, _row_chunk(x_ref.shape[0]))


def _ffn_body(*refs, n_p, n_ride, pages_per_blk, **kw):
    if n_ride:
        refs = refs[1:]
    xp_ref, xs_ref, wg_ref, wu_ref, wd_ref, g_ref, b_ref = refs[:7]
    page_refs = refs[7:7 + n_ride]
    op_ref, os_ref = refs[7 + n_ride:9 + n_ride]
    means_ref = refs[9 + n_ride] if n_ride else None
    xbp_ref, xbs_ref = refs[-2:]
    i = pl.program_id(0)
    j = pl.program_id(1)

    def side_work():
        if n_ride:
            _block_means(page_refs, means_ref, pages_per_blk)

    _ffn_tile(xp_ref, wg_ref, wu_ref, wd_ref, g_ref, b_ref, op_ref, xbp_ref, j, side_work, **kw)

    @pl.when(i == n_p - 1)
    def _():
        _ffn_tile(xs_ref, wg_ref, wu_ref, wd_ref, g_ref, b_ref, os_ref, xbs_ref, j, lambda: None, **kw)


RIDE_BLOCKS_PER_STEP = 2


def _ffn(xp, xs, w_gate, w_up, w_down, lng, lnb, layer, sub, ln_idx, *, alpha, tm, tf, ride=None):
    mp, d = xp.shape
    ms = xs.shape[0]
    f = w_gate.shape[-1]
    n_p, n_j = mp // tm, f // tf
    prompt_tile = pl.BlockSpec((tm, d), lambda i, j, *_: (i, 0))
    sample_tile = pl.BlockSpec((ms, d), lambda i, j, *_: (0, 0))
    in_specs = [
        prompt_tile,
        sample_tile,
        pl.BlockSpec((None, None, d, tf), lambda i, j, *_: (layer, sub, 0, j)),
        pl.BlockSpec((None, None, d, tf), lambda i, j, *_: (layer, sub, 0, j)),
        pl.BlockSpec((None, None, tf, d), lambda i, j, *_: (layer, sub, j, 0)),
        pl.BlockSpec((None, 1, d), lambda i, j, *_: (ln_idx, 0, 0)),
        pl.BlockSpec((None, 1, d), lambda i, j, *_: (ln_idx, 0, 0)),
    ]
    out_specs = [prompt_tile, sample_tile]
    out_shape = [jax.ShapeDtypeStruct((mp, d), F32), jax.ShapeDtypeStruct((ms, d), F32)]
    operands = [xp, xs, w_gate, w_up, w_down, lng, lnb]
    n_ride = ppb = 0
    prefetch = []
    if ride is not None:
        cache_k, page_table, first_block = ride
        _, page, nh, dh = cache_k.shape
        n_pages = page_table.shape[1]
        ppb = MOBA_BLOCK // page
        bps = RIDE_BLOCKS_PER_STEP
        n_here = min(n_p * n_j * bps, page_table.size // ppb - first_block)
        assert n_here > 0 and n_here % bps == 0
        last_step = n_here // bps - 1
        n_ride = bps * ppb

        def step_of(i, j):
            return jnp.minimum(i * n_j + j, last_step)

        def page_spec(k):
            def index(i, j, pt):
                pg = (first_block + step_of(i, j) * bps) * ppb + k
                return (pt[pg // n_pages, pg % n_pages], 0, 0, 0)
            return pl.BlockSpec((None, page, nh, dh), index)

        in_specs += [page_spec(k) for k in range(n_ride)]
        operands += [cache_k] * n_ride
        out_specs.append(pl.BlockSpec((bps, nh, dh), lambda i, j, pt: (step_of(i, j), 0, 0)))
        out_shape.append(jax.ShapeDtypeStruct((n_here, nh, dh), F32))
        prefetch = [page_table]
    return pl.pallas_call(
        functools.partial(_ffn_body, alpha=alpha, n_j=n_j, n_p=n_p, n_ride=n_ride, pages_per_blk=ppb),
        grid_spec=pltpu.PrefetchScalarGridSpec(
            num_scalar_prefetch=len(prefetch),
            grid=(n_p, n_j),
            in_specs=in_specs,
            out_specs=out_specs,
            scratch_shapes=[pltpu.VMEM((tm, d), BF16), pltpu.VMEM((ms, d), BF16)],
        ),
        out_shape=out_shape,
        compiler_params=_params("arbitrary", "arbitrary"),
        name="ffn",
    )(*prefetch, *operands)


def _conv_body(*refs, alpha, n_j, tiles_per_seq, seq_len, multi_seq):
    if multi_seq:
        (x_ref, wb_ref, wc_ref, wh_ref, wk_ref, wo_ref, g_ref, b_ref, p1_ref, p2_ref,
         o_ref, tail_ref, xb_ref, halo_ref) = refs
    else:
        (x_ref, wb_ref, wc_ref, wh_ref, wk_ref, wo_ref, g_ref, b_ref, prev_ref,
         o_ref, tail_ref, xb_ref, halo_ref) = refs
    i = pl.program_id(0)
    j = pl.program_id(1)
    tm = x_ref.shape[0]
    tail_rows = tail_ref.shape[0]

    _start_tile(x_ref, xb_ref, o_ref, j)
    if not multi_seq:
        @pl.when((i % tiles_per_seq) == 0)
        def _():
            halo_ref[j] = prev_ref[...]

    xb = xb_ref[...]
    gate_b = jnp.dot(xb, wb_ref[...].astype(BF16), preferred_element_type=F32)
    gate_c = jnp.dot(xb, wc_ref[...].astype(BF16), preferred_element_type=F32)
    whb = wh_ref[...].astype(BF16)
    half = tm // 2 if tm % (4 * SUBLANES) == 0 else tm
    hh = jnp.concatenate([jnp.dot(xb[r:r + half], whb, preferred_element_type=F32)
                          for r in range(0, tm, half)], axis=0)
    u = gate_c * hh
    s1 = pltpu.roll(u, 1, axis=0)
    s2 = pltpu.roll(u, 2, axis=0)
    if multi_seq:
        t = lax.broadcasted_iota(jnp.int32, u.shape, 0) % seq_len
        s1 = jnp.where(t >= 1, s1, p1_ref[...])
        s2 = jnp.where(t >= 2, s2, p2_ref[...])
    else:
        halo = halo_ref[j]
        row = lax.broadcasted_iota(jnp.int32, halo.shape, 0)
        before1 = jnp.broadcast_to(halo[SUBLANES - 1:SUBLANES, :], halo.shape)
        before2 = jnp.broadcast_to(halo[SUBLANES - 2:SUBLANES - 1, :], halo.shape)
        top1 = jnp.where(row == 0, before1, s1[0:SUBLANES, :])
        top2 = jnp.where(row == 0, before2, jnp.where(row == 1, before1, s2[0:SUBLANES, :]))
        s1 = jnp.concatenate([top1, s1[SUBLANES:, :]], axis=0)
        s2 = jnp.concatenate([top2, s2[SUBLANES:, :]], axis=0)
    wk = wk_ref[...]
    y = wk[0:1, :] * s2
    y = y + wk[1:2, :] * s1
    y = y + wk[2:3, :] * u
    yb = (gate_b * y).astype(BF16)
    o_ref[...] += jnp.dot(yb, wo_ref[...].astype(BF16), preferred_element_type=F32)

    halo_ref[j] = u[tm - SUBLANES:, :]
    tail_ref[...] = u[tm - tail_rows:, :]

    @pl.when(j == n_j - 1)
    def _():
        _post_norm_rows(x_ref, o_ref, g_ref, b_ref, alpha, 1.0, _row_chunk(tm))


def _conv(x, w_in, w_conv, w_out, lng, lnb, layer, ln_idx, prev, *, alpha, tm, tn, seq_len):
    m, d = x.shape
    n_i, n_j = m // tm, d // tn
    multi_seq = seq_len < tm
    tiles_per_seq = max(seq_len // tm, 1)
    tail_rows = tm if multi_seq else SUBLANES
    in_specs = [
        pl.BlockSpec((tm, d), lambda i, j: (i, 0)),
        pl.BlockSpec((None, d, tn), lambda i, j: (layer, 0, j)),
        pl.BlockSpec((None, d, tn), lambda i, j: (layer, 0, n_j + j)),
        pl.BlockSpec((None, d, tn), lambda i, j: (layer, 0, 2 * n_j + j)),
        pl.BlockSpec((None, CONV_W, tn), lambda i, j: (layer, 0, j)),
        pl.BlockSpec((None, tn, d), lambda i, j: (layer, j, 0)),
        pl.BlockSpec((None, 1, d), lambda i, j: (ln_idx, 0, 0)),
        pl.BlockSpec((None, 1, d), lambda i, j: (ln_idx, 0, 0)),
    ]
    if multi_seq:
        in_specs += [pl.BlockSpec((tm, tn), lambda i, j: (i, j))] * 2
        extra = tuple(prev)
    else:
        in_specs += [pl.BlockSpec((None, SUBLANES, tn), lambda i, j: (i // tiles_per_seq, 0, j))]
        extra = (prev,)
    return pl.pallas_call(
        functools.partial(_conv_body, alpha=alpha, n_j=n_j, tiles_per_seq=tiles_per_seq,
                          seq_len=seq_len, multi_seq=multi_seq),
        grid=(n_i, n_j),
        in_specs=in_specs,
        out_specs=[
            pl.BlockSpec((tm, d), lambda i, j: (i, 0)),
            pl.BlockSpec((None, tail_rows, tn), lambda i, j: (i, 0, j)),
        ],
        out_shape=[
            jax.ShapeDtypeStruct((m, d), F32),
            jax.ShapeDtypeStruct((n_i, tail_rows, d), F32),
        ],
        scratch_shapes=[
            pltpu.VMEM((tm, d), BF16),
            pltpu.VMEM((n_j, SUBLANES, tn), F32),
        ],
        compiler_params=_params("arbitrary", "arbitrary"),
        name="conv",
    )(x, w_in, w_in, w_in, w_conv, w_out, lng, lnb, *extra)


def _rope_cols(y, cos, sin_signed):
    heads = []
    for c in range(y.shape[1] // LANES):
        yh = y[:, c * LANES:(c + 1) * LANES]
        heads.append(yh * cos + pltpu.roll(yh, LANES // 2, axis=1) * sin_signed)
    return heads[0] if len(heads) == 1 else jnp.concatenate(heads, axis=1)


def _proj_body(*refs, rope):
    n_w = len(rope)
    x_ref = refs[0]
    w_refs = refs[1:1 + n_w]
    cos_ref, sin_ref = refs[1 + n_w:3 + n_w]
    o_refs = refs[3 + n_w:3 + 2 * n_w]
    xb_ref = refs[3 + 2 * n_w]
    j = pl.program_id(1)

    @pl.when(j == 0)
    def _():
        _cast_rows(x_ref, xb_ref, _row_chunk(x_ref.shape[0]))

    xb = xb_ref[...]
    for w_ref, o_ref, use_rope in zip(w_refs, o_refs, rope):
        y = jnp.dot(xb, w_ref[...].astype(BF16), preferred_element_type=F32)
        if use_rope:
            y = _rope_cols(y, cos_ref[...], sin_ref[...])
        o_ref[...] = y


def _proj(x, weights, rope, cos, sin_signed, *, tm, tn):
    m, d = x.shape
    n = weights[0][0].shape[-1]
    n_i, n_j = m // tm, n // tn
    pos_tiles = cos.shape[0] // tm
    w_specs = []
    for w, lead in weights:
        if lead is None:
            w_specs.append(pl.BlockSpec((d, tn), lambda i, j: (0, j)))
        else:
            w_specs.append(pl.BlockSpec((None, d, tn), lambda i, j, lead=lead: (lead, 0, j)))
    pos_spec = pl.BlockSpec((tm, LANES), lambda i, j: (i % pos_tiles, 0))
    return pl.pallas_call(
        functools.partial(_proj_body, rope=tuple(rope)),
        grid=(n_i, n_j),
        in_specs=[pl.BlockSpec((tm, d), lambda i, j: (i, 0))] + w_specs + [pos_spec, pos_spec],
        out_specs=[pl.BlockSpec((tm, tn), lambda i, j: (i, j))] * len(weights),
        out_shape=[jax.ShapeDtypeStruct((m, n), F32)] * len(weights),
        scratch_shapes=[pltpu.VMEM((tm, d), BF16)],
        compiler_params=_params("parallel", "arbitrary"),
        name="proj",
    )(x, *[w for w, _ in weights], cos, sin_signed)


def _oproj_body(x_ref, a_ref, w_ref, g_ref, b_ref, o_ref, *, alpha, n_j):
    j = pl.program_id(1)
    _start_tile(x_ref, None, o_ref, j)
    o_ref[...] += jnp.dot(a_ref[...], w_ref[...].astype(BF16), preferred_element_type=F32)

    @pl.when(j == n_j - 1)
    def _():
        _post_norm_rows(x_ref, o_ref, g_ref, b_ref, alpha, 1.0, _row_chunk(x_ref.shape[0]))


def _oproj(x, a, w_o, lng, lnb, layer, ln_idx, *, alpha, tm, tk):
    m, d = x.shape
    hd = a.shape[1]
    n_i, n_j = m // tm, hd // tk
    return pl.pallas_call(
        functools.partial(_oproj_body, alpha=alpha, n_j=n_j),
        grid=(n_i, n_j),
        in_specs=[
            pl.BlockSpec((tm, d), lambda i, j: (i, 0)),
            pl.BlockSpec((tm, tk), lambda i, j: (i, j)),
            pl.BlockSpec((None, tk, d), lambda i, j: (layer, j, 0)),
            pl.BlockSpec((None, 1, d), lambda i, j: (ln_idx, 0, 0)),
            pl.BlockSpec((None, 1, d), lambda i, j: (ln_idx, 0, 0)),
        ],
        out_specs=pl.BlockSpec((tm, d), lambda i, j: (i, 0)),
        out_shape=jax.ShapeDtypeStruct((m, d), F32),
        compiler_params=_params("parallel", "arbitrary"),
        name="oproj",
    )(x, a, w_o, lng, lnb)


def _select_blocks(gate_t, n_valid, topk):
    nb = gate_t.shape[0]
    blk = lax.broadcasted_iota(jnp.int32, gate_t.shape, 0)
    g = jnp.where(blk < n_valid, gate_t, NEG_INF)
    sel = jnp.zeros(gate_t.shape, jnp.bool_)
    for s in range(topk):
        best = jnp.max(g, axis=0, keepdims=True)
        idx = jnp.min(jnp.where(g == best, blk, nb), axis=0, keepdims=True)
        hit = blk == idx
        sel = jnp.logical_or(sel, jnp.logical_and(hit, s < n_valid))
        g = jnp.where(hit, -jnp.inf, g)
    return sel


def _attn_body(q_ref, k_ref, v_ref, o_ref, means_ref, kb_ref, vt_ref, bias_ref, s_ref, p_ref,
               *, scale, hps):
    i = pl.program_id(2)
    blk = MOBA_BLOCK
    nb = k_ref.shape[0] // blk
    dh = LANES

    @pl.when(i == 0)
    def _():
        for hh in range(hps):
            cols = slice(hh * dh, (hh + 1) * dh)
            for n in range(nb):
                rows = slice(n * blk, (n + 1) * blk)
                kf = k_ref[rows, cols]
                means_ref[hh, n:n + 1, :] = jnp.sum(kf, axis=0, keepdims=True) * (1.0 / blk)
                kb_ref[hh, n] = kf.astype(BF16)
                vt_ref[hh, n] = v_ref[rows, cols].T.astype(BF16)

    key = lax.broadcasted_iota(jnp.int32, (blk, blk), 0)
    qry = lax.broadcasted_iota(jnp.int32, (blk, blk), 1)
    qts = [q_ref[:, hh * dh:(hh + 1) * dh].T for hh in range(hps)]
    qbs = [(qt * (scale * LOG2_E)).astype(BF16) for qt in qts]

    def logits(n):
        return [jnp.dot(kb_ref[hh, n], qbs[hh], preferred_element_type=F32) for hh in range(hps)]

    own, first_past = logits(i), logits(0)
    for hh in range(hps):
        gate_t = jnp.dot(means_ref[hh], qts[hh], precision=lax.Precision.HIGHEST,
                         preferred_element_type=F32)
        bias_ref[hh] = jnp.where(_select_blocks(gate_t, i, MOBA_TOPK), 0.0, NEG_INF)
    carry = []
    for hh in range(hps):
        s = jnp.where(key <= qry, own[hh], NEG_INF)
        m = jnp.max(s, axis=0, keepdims=True)
        p = jnp.exp2(s - m)
        p_ref[hh] = p.astype(BF16)
        s_ref[hh] = first_past[hh]
        carry += [m, jnp.sum(p, axis=0, keepdims=True), jnp.zeros((dh, blk), F32),
                  jnp.ones((1, blk), F32)]

    def pv_products(v_blk):
        return [jnp.dot(vt_ref[hh, v_blk], p_ref[hh], preferred_element_type=F32)
                for hh in range(hps)]

    def past(n, carry):
        ss_next = logits(jnp.minimum(n + 1, nb - 1))
        pvs = pv_products(jnp.where(n == 0, i, n - 1))
        out, ps = [], []
        for hh in range(hps):
            m, l, acc, corr_prev = carry[4 * hh:4 * hh + 4]
            s = s_ref[hh]
            bias = bias_ref[hh, pl.ds(n, 1), :]
            m_new = jnp.maximum(m, jnp.max(s, axis=0, keepdims=True) + bias)
            p = jnp.exp2(s + (bias - m_new))
            corr = jnp.exp2(m - m_new)
            ps.append(p.astype(BF16))
            out += [m_new, l * corr + jnp.sum(p, axis=0, keepdims=True),
                    acc * corr_prev + pvs[hh], corr]
        for hh in range(hps):
            p_ref[hh] = ps[hh]
            s_ref[hh] = ss_next[hh]
        return tuple(out)

    carry = lax.fori_loop(0, i, past, tuple(carry))
    pvs = pv_products(jnp.where(i == 0, i, i - 1))
    for hh in range(hps):
        _, l, acc, corr_prev = carry[4 * hh:4 * hh + 4]
        acc = acc * corr_prev + pvs[hh]
        o_ref[:, hh * dh:(hh + 1) * dh] = (acc / l).T.astype(o_ref.dtype)


def _moba_prompt(q, k, v, *, n_heads, hps):
    b, t, hd = q.shape
    dh = hd // n_heads
    nb = t // MOBA_BLOCK
    w = hps * dh
    return pl.pallas_call(
        functools.partial(_attn_body, scale=dh ** -0.5, hps=hps),
        grid=(b, n_heads // hps, nb),
        in_specs=[
            pl.BlockSpec((None, MOBA_BLOCK, w), lambda bi, h, i: (bi, i, h)),
            pl.BlockSpec((None, t, w), lambda bi, h, i: (bi, 0, h)),
            pl.BlockSpec((None, t, w), lambda bi, h, i: (bi, 0, h)),
        ],
        out_specs=pl.BlockSpec((None, MOBA_BLOCK, w), lambda bi, h, i: (bi, i, h)),
        out_shape=jax.ShapeDtypeStruct((b, t, hd), BF16),
        scratch_shapes=[
            pltpu.VMEM((hps, nb, dh), F32),
            pltpu.VMEM((hps, nb, MOBA_BLOCK, dh), BF16),
            pltpu.VMEM((hps, nb, dh, MOBA_BLOCK), BF16),
            pltpu.VMEM((hps, nb, MOBA_BLOCK), F32),
            pltpu.VMEM((hps, MOBA_BLOCK, MOBA_BLOCK), F32),
            pltpu.VMEM((hps, MOBA_BLOCK, MOBA_BLOCK), BF16),
        ],
        compiler_params=_params("parallel", "parallel", "arbitrary"),
        name="moba_prompt",
    )(q, k, v)


def _means_body(pt_ref, *refs, pages_per_blk):
    _block_means(refs[:-1], refs[-1], pages_per_blk)


def _cache_means(cache_k, page_table, *, blocks_per_step):
    _, page, nh, dh = cache_k.shape
    b, n_pages = page_table.shape
    ppb = MOBA_BLOCK // page
    pps = ppb * blocks_per_step
    page_specs = [
        pl.BlockSpec((None, page, nh, dh), lambda bi, s, pt, k=k: (pt[bi, s * pps + k], 0, 0, 0))
        for k in range(pps)
    ]
    return pl.pallas_call(
        functools.partial(_means_body, pages_per_blk=ppb),
        grid_spec=pltpu.PrefetchScalarGridSpec(
            num_scalar_prefetch=1,
            grid=(b, n_pages // pps),
            in_specs=page_specs,
            out_specs=pl.BlockSpec((None, blocks_per_step, nh, dh), lambda bi, s, pt: (bi, s, 0, 0)),
        ),
        out_shape=jax.ShapeDtypeStruct((b, n_pages // ppb, nh, dh), F32),
        compiler_params=_params("parallel", "arbitrary"),
        name="cache_means",
    )(page_table, *([cache_k] * pps))


def _sgate_body(means_ref, q_ref, o_ref, *, topk):
    nq = q_ref.shape[0]
    nb, nh, _ = means_ref.shape
    o_ref[...] = jnp.zeros(o_ref.shape, jnp.int32)
    means = means_ref[...]
    blk = lax.broadcasted_iota(jnp.int32, (nb, nh), 0)
    for qi in range(nq):
        g = jnp.sum(means * q_ref[qi][None], axis=-1)
        for s in range(topk):
            best = jnp.max(g, axis=0, keepdims=True)
            idx = jnp.min(jnp.where(g == best, blk, nb), axis=0, keepdims=True)
            o_ref[qi, s:s + 1, 0:nh] = idx
            g = jnp.where(blk == idx, -jnp.inf, g)


def _sample_select(means, q4, *, topk):
    b, nb, nh, dh = means.shape
    nq = q4.shape[1]
    return pl.pallas_call(
        functools.partial(_sgate_body, topk=topk),
        grid=(b,),
        in_specs=[
            pl.BlockSpec((None, nb, nh, dh), lambda bi: (bi, 0, 0, 0)),
            pl.BlockSpec((None, nq, nh, dh), lambda bi: (bi, 0, 0, 0)),
        ],
        out_specs=pl.BlockSpec((None, nq, SUBLANES, LANES), lambda bi: (bi, 0, 0, 0)),
        out_shape=jax.ShapeDtypeStruct((b, nq, SUBLANES, LANES), jnp.int32),
        compiler_params=_params("parallel"),
        name="sample_select",
    )(means, q4)


def _sattn_body(pt_ref, idx_ref, q_ref, kn_ref, vn_ref, ck_ref, cv_ref, o_ref,
                kg_ref, vg_ref, sem, *, nq, topk, page, scale):
    b = pl.program_id(0)
    h = pl.program_id(1)
    nh = pl.num_programs(1)
    ppb = MOBA_BLOCK // page
    n_slots = nq * topk

    step = b * nh + h
    buf = lax.rem(step, 2)

    def copies(step_b, step_h, dst, slot, half):
        blk = idx_ref[(step_b * nh + step_h) * n_slots + slot]
        pg = pt_ref[step_b, blk * ppb + half]
        rows = pl.ds(slot * MOBA_BLOCK + half * page, page)
        return (pltpu.make_async_copy(ck_ref.at[pg, :, step_h, :], kg_ref.at[dst, rows, :], sem.at[dst, 0]),
                pltpu.make_async_copy(cv_ref.at[pg, :, step_h, :], vg_ref.at[dst, rows, :], sem.at[dst, 1]))

    def for_all_copies(step_b, step_h, dst, fn):
        for slot in range(n_slots):
            for half in range(ppb):
                for cp in copies(step_b, step_h, dst, slot, half):
                    fn(cp)

    @pl.when(step == 0)
    def _():
        for_all_copies(b, h, buf, lambda cp: cp.start())

    @pl.when(step + 1 < pl.num_programs(0) * nh)
    def _():
        wrap = h + 1 == nh
        for_all_copies(jnp.where(wrap, b + 1, b), jnp.where(wrap, 0, h + 1), 1 - buf,
                       lambda cp: cp.start())

    for_all_copies(b, h, buf, lambda cp: cp.wait())

    rows = q_ref.shape[0]
    qb = q_ref[...].astype(BF16)
    n_keys = n_slots * MOBA_BLOCK
    s_past = lax.dot_general(qb, kg_ref[buf].astype(BF16), _NT, preferred_element_type=F32) * scale
    s_new = lax.dot_general(qb, kn_ref[...].astype(BF16), _NT, preferred_element_type=F32) * scale
    r_past = lax.broadcasted_iota(jnp.int32, (rows, n_keys), 0)
    c_past = lax.broadcasted_iota(jnp.int32, (rows, n_keys), 1)
    lo = r_past * (topk * MOBA_BLOCK)
    m_past = jnp.logical_and(c_past >= lo, c_past < lo + topk * MOBA_BLOCK)
    r_new = lax.broadcasted_iota(jnp.int32, (rows, rows), 0)
    c_new = lax.broadcasted_iota(jnp.int32, (rows, rows), 1)
    m_new = jnp.logical_and(c_new <= r_new, c_new < nq)
    s_past = jnp.where(m_past, s_past, NEG_INF)
    s_new = jnp.where(m_new, s_new, NEG_INF)
    top = jnp.maximum(jnp.max(s_past, axis=1, keepdims=True), jnp.max(s_new, axis=1, keepdims=True))
    p_past = jnp.where(m_past, jnp.exp(s_past - top), 0.0)
    p_new = jnp.where(m_new, jnp.exp(s_new - top), 0.0)
    l = jnp.sum(p_past, axis=1, keepdims=True) + jnp.sum(p_new, axis=1, keepdims=True)
    acc = jnp.dot(p_past.astype(BF16), vg_ref[buf].astype(BF16), preferred_element_type=F32)
    acc = acc + jnp.dot(p_new.astype(BF16), vn_ref[...].astype(BF16), preferred_element_type=F32)
    o_ref[...] = (acc / l).astype(o_ref.dtype)


def _moba_sample(q8, k8, v8, cache_k, cache_v, page_table, sel_idx, *, nq, n_heads):
    b, rows, hd = q8.shape
    dh = hd // n_heads
    page = cache_k.shape[1]
    n_keys = nq * MOBA_TOPK * MOBA_BLOCK
    blk3 = pl.BlockSpec((None, rows, dh), lambda bi, h, pt, idx: (bi, 0, h))
    return pl.pallas_call(
        functools.partial(_sattn_body, nq=nq, topk=MOBA_TOPK, page=page, scale=dh ** -0.5),
        grid_spec=pltpu.PrefetchScalarGridSpec(
            num_scalar_prefetch=2,
            grid=(b, n_heads),
            in_specs=[blk3, blk3, blk3,
                      pl.BlockSpec(memory_space=pl.ANY), pl.BlockSpec(memory_space=pl.ANY)],
            out_specs=blk3,
            scratch_shapes=[
                pltpu.VMEM((2, n_keys, dh), F32),
                pltpu.VMEM((2, n_keys, dh), F32),
                pltpu.SemaphoreType.DMA((2, 2)),
            ],
        ),
        out_shape=jax.ShapeDtypeStruct((b, rows, hd), BF16),
        compiler_params=_params("arbitrary", "arbitrary"),
        name="moba_sample",
    )(page_table, sel_idx, q8, k8, v8, cache_k, cache_v)


def _rope_tables(pos, dh):
    half = dh // 2
    inv = ROPE_THETA ** (-2.0 * jnp.arange(half, dtype=F32) / dh)
    ang = pos.astype(F32)[:, None] * inv[None, :]
    cos, sin = jnp.cos(ang), jnp.sin(ang)
    return jnp.concatenate([cos, cos], axis=1), jnp.concatenate([-sin, sin], axis=1)


def _pad_rows(a, rows):
    return jnp.pad(a, ((0, 0), (0, rows - a.shape[1]), (0, 0)))


def _trunk(x, n_seq, seq_len, pos, conv_prev, past, w, *, tm, tf, tn):
    (lng, lnb, w_ffn_gate, w_ffn_up, w_ffn_down, w_conv_in, w_conv, w_conv_out,
     w_k, w_v, w_q, w_o) = w
    depth = w_ffn_gate.shape[0]
    n_a = w_conv_in.shape[0]
    alpha = float((2 * depth) ** 0.25)
    m, d = x.shape
    n_heads = w_k.shape[1] // LANES
    dh = w_k.shape[1] // n_heads
    cos, sin_signed = _rope_tables(pos, dh)
    if past is not None:
        cos, sin_signed = jnp.tile(cos, (n_seq, 1)), jnp.tile(sin_signed, (n_seq, 1))
    wide = 2 * tn if w_k.shape[1] % (2 * tn) == 0 else tn
    conv_states = []
    for layer in range(depth):
        if layer == n_a:
            k_new, v_new = _proj(x, [(w_k, None), (w_v, None)], (True, False), cos, sin_signed,
                                 tm=tm, tn=wide)
            if past is not None:
                means = past[3]()
        x = yield x, layer, 0, 3 * layer
        if layer < n_a:
            if seq_len < tm:
                prev = conv_prev[layer]
                p1 = jnp.zeros((n_seq, seq_len, d), F32).at[:, 0].set(prev[:, 1])
                p2 = jnp.zeros((n_seq, seq_len, d), F32).at[:, 0].set(prev[:, 0]).at[:, 1].set(prev[:, 1])
                prev_arg = (p1.reshape(m, d), p2.reshape(m, d))
            else:
                prev_arg = jnp.pad(conv_prev[layer], ((0, 0), (SUBLANES - (CONV_W - 1), 0), (0, 0)))
            x, tail = _conv(x, w_conv_in, w_conv, w_conv_out, lng, lnb, layer, 3 * layer + 1,
                            prev_arg, alpha=alpha, tm=tm, tn=tn, seq_len=seq_len)
            if seq_len < tm:
                st = tail.reshape(n_seq, seq_len, d)[:, seq_len - (CONV_W - 1):]
            else:
                tps = seq_len // tm
                st = tail[tps - 1::tps, SUBLANES - (CONV_W - 1):]
            conv_states.append(st)
        else:
            jb = layer - n_a
            (q,) = _proj(x, [(w_q, jb)], (True,), cos, sin_signed, tm=tm, tn=wide)
            if past is None:
                a = _moba_prompt(q.reshape(n_seq, seq_len, -1), k_new.reshape(n_seq, seq_len, -1),
                                 v_new.reshape(n_seq, seq_len, -1), n_heads=n_heads,
                                 hps=4 if n_heads % 4 == 0 else 1)
                a = a.reshape(m, -1)
            else:
                q3 = q.reshape(n_seq, seq_len, -1)
                sel = _sample_select(means, q3.reshape(n_seq, seq_len, n_heads, dh), topk=MOBA_TOPK)
                sel = sel[:, :, :MOBA_TOPK, :n_heads].transpose(0, 3, 1, 2).reshape(-1)
                a8 = _moba_sample(_pad_rows(q3, SUBLANES),
                                  _pad_rows(k_new.reshape(n_seq, seq_len, -1), SUBLANES),
                                  _pad_rows(v_new.reshape(n_seq, seq_len, -1), SUBLANES),
                                  past[0], past[1], past[2], sel, nq=seq_len, n_heads=n_heads)
                a = a8[:, :seq_len].reshape(m, -1)
            x = _oproj(x, a, w_o, lng, lnb, jb, 3 * layer + 1, alpha=alpha, tm=tm, tk=wide)
        x = yield x, layer, 1, 3 * layer + 2
    return x, jnp.stack(conv_states), k_new, v_new


def kernel(x_prompt, x_sample, state_conv, cache_k, cache_v, page_table, ln_g, ln_b, w_ffn_gate, w_ffn_up, w_ffn_down, w_conv_in, w_conv, w_conv_out, w_k, w_v, w_q, w_o):
    bp, tp, d = x_prompt.shape
    bs, ts, _ = x_sample.shape
    n_heads, dh = cache_k.shape[2], cache_k.shape[3]
    n_a = w_conv_in.shape[0]
    past_len = page_table.shape[1] * cache_k.shape[1]
    assert dh == LANES and w_conv.shape[1] == CONV_W
    assert tp % MOBA_BLOCK == 0 and past_len % MOBA_BLOCK == 0 and MOBA_BLOCK % cache_k.shape[1] == 0
    assert CONV_W - 1 <= ts <= SUBLANES and past_len // MOBA_BLOCK >= MOBA_TOPK
    lng = ln_g.reshape(-1, 1, d)
    lnb = ln_b.reshape(-1, 1, d)
    w = (lng, lnb, w_ffn_gate, w_ffn_up, w_ffn_down, w_conv_in, w_conv, w_conv_out, w_k, w_v, w_q, w_o)
    f = w_ffn_gate.shape[-1]
    tf = 256 if f % 256 == 0 else f
    tn = 256 if d % 256 == 0 else d

    conv0 = jnp.zeros((n_a, bp, CONV_W - 1, d), F32)
    tm_p = min(1024, tp)
    pos_s = past_len + jnp.arange(ts, dtype=jnp.int32)
    prompt = _trunk(x_prompt.reshape(bp * tp, d), bp, tp, jnp.arange(tp, dtype=jnp.int32),
                    conv0, None, w, tm=tm_p, tf=tf, tn=tn)
    n_blocks = past_len // MOBA_BLOCK
    mean_parts = []

    def block_means():
        if sum(part.shape[0] for part in mean_parts) == bs * n_blocks:
            return jnp.concatenate(mean_parts, axis=0).reshape(bs, n_blocks, n_heads, dh)
        return _cache_means(cache_k, page_table, blocks_per_step=2 if n_blocks % 2 == 0 else 1)

    sample = _trunk(x_sample.reshape(bs * ts, d), bs, ts, pos_s, state_conv,
                    (cache_k, cache_v, page_table, block_means), w, tm=bs * ts, tf=tf, tn=tn)
    alpha = float((2 * w_ffn_gate.shape[0]) ** 0.25)
    req_p, req_s = next(prompt), next(sample)
    results = None
    while results is None:
        (xp, layer, sub, ln_idx), xs = req_p, req_s[0]
        first_block = sum(part.shape[0] for part in mean_parts)
        ride = (cache_k, page_table, first_block) if first_block < bs * n_blocks else None
        yp, ys, *part = _ffn(xp, xs, w_ffn_gate, w_ffn_up, w_ffn_down, lng, lnb, layer, sub, ln_idx,
                             alpha=alpha, tm=tm_p, tf=tf, ride=ride)
        mean_parts += part
        try:
            req_p, req_s = prompt.send(yp), sample.send(ys)
        except StopIteration as done:
            try:
                sample.send(ys)
            except StopIteration as done_s:
                results = done.value, done_s.value
    (y_p, conv_p, k_p, v_p), (y_s, conv_s, k_s, v_s) = results
    return (y_p.reshape(bp, tp, d), y_s.reshape(bs, ts, d), conv_p, conv_s,
            k_p.reshape(bp, tp, n_heads, dh), v_p.reshape(bp, tp, n_heads, dh),
            k_s.reshape(bs, ts, n_heads, dh), v_s.reshape(bs, ts, n_heads, dh))
```

```python
import functools

import jax
import jax.numpy as jnp
from jax import lax
from jax.experimental import pallas as pl
from jax.experimental.pallas import tpu as pltpu

F32 = jnp.float32
BF16 = jnp.bfloat16

MOBA_BLOCK = 256
MOBA_TOPK = 3
CONV_W = 3
ROPE_THETA = 10000.0
LN_EPS = 1e-5
NEG_INF = -1e30
LOG2_E = 1.4426950408889634

SUBLANES = 8
LANES = 128
VMEM_LIMIT_BYTES = 60 * 1024 * 1024

_NT = (((1,), (1,)), ((), ()))


def _params(*sem):
    return pltpu.CompilerParams(dimension_semantics=sem, vmem_limit_bytes=VMEM_LIMIT_BYTES)


def _layer_norm(z, g, b):
    mu = jnp.mean(z, axis=-1, keepdims=True)
    d = z - mu
    var = jnp.mean(d * d, axis=-1, keepdims=True)
    return d * lax.rsqrt(var + LN_EPS) * g + b


def _cast_rows(src_ref, dst_ref, rows):
    n = src_ref.shape[0] // rows

    def body(c, carry):
        r = pl.ds(pl.multiple_of(c * rows, rows), rows)
        dst_ref[r, :] = src_ref[r, :].astype(dst_ref.dtype)
        return carry

    lax.fori_loop(0, n, body, 0)


def _start_tile(x_ref, xb_ref, o_ref, j):
    @pl.when(j == 0)
    def _():
        if xb_ref is not None:
            _cast_rows(x_ref, xb_ref, _row_chunk(x_ref.shape[0]))
        o_ref[...] = jnp.zeros(o_ref.shape, o_ref.dtype)


def _post_norm_rows(x_ref, o_ref, g_ref, b_ref, alpha, scale, rows):
    n = o_ref.shape[0] // rows
    g = g_ref[...]
    b = b_ref[...]

    def body(c, carry):
        r = pl.ds(pl.multiple_of(c * rows, rows), rows)
        z = alpha * x_ref[r, :] + scale * o_ref[r, :]
        o_ref[r, :] = _layer_norm(z, g, b)
        return carry

    lax.fori_loop(0, n, body, 0)


def _row_chunk(tm):
    return min(tm, 256)


def _block_means(page_refs, means_ref, pages_per_blk):
    for blk in range(len(page_refs) // pages_per_blk):
        total = jnp.sum(page_refs[blk * pages_per_blk][...], axis=0)
        for p in range(1, pages_per_blk):
            total = total + jnp.sum(page_refs[blk * pages_per_blk + p][...], axis=0)
        means_ref[blk] = total * (1.0 / MOBA_BLOCK)


def _ffn_tile(x_ref, wg_ref, wu_ref, wd_ref, g_ref, b_ref, o_ref, xb_ref, j, side_work,
              *, alpha, n_j):
    _start_tile(x_ref, xb_ref, o_ref, j)
    xb = xb_ref[...]
    gate = jnp.dot(xb, wg_ref[...].astype(BF16), preferred_element_type=F32)
    up = jnp.dot(xb, wu_ref[...].astype(BF16), preferred_element_type=F32)
    side_work()
    h = (jax.nn.silu(gate) * up).astype(BF16)
    o_ref[...] += jnp.dot(h, wd_ref[...].astype(BF16), preferred_element_type=F32)

    @pl.when(j == n_j - 1)
    def _():
        _post_norm_rows(x_ref, o_ref, g_ref, b_ref, alpha, 0.5, _row_chunk(x_ref.shape[0]))


def _ffn_body(*refs, n_p, n_ride, pages_per_blk, **kw):
    if n_ride:
        refs = refs[1:]
    xp_ref, xs_ref, wg_ref, wu_ref, wd_ref, g_ref, b_ref = refs[:7]
    page_refs = refs[7:7 + n_ride]
    op_ref, os_ref = refs[7 + n_ride:9 + n_ride]
    means_ref = refs[9 + n_ride] if n_ride else None
    xbp_ref, xbs_ref = refs[-2:]
    i = pl.program_id(0)
    j = pl.program_id(1)

    def side_work():
        if n_ride:
            _block_means(page_refs, means_ref, pages_per_blk)

    _ffn_tile(xp_ref, wg_ref, wu_ref, wd_ref, g_ref, b_ref, op_ref, xbp_ref, j, side_work, **kw)

    @pl.when(i == n_p - 1)
    def _():
        _ffn_tile(xs_ref, wg_ref, wu_ref, wd_ref, g_ref, b_ref, os_ref, xbs_ref, j, lambda: None, **kw)


RIDE_BLOCKS_PER_STEP = 2


def _ffn(xp, xs, w_gate, w_up, w_down, lng, lnb, layer, sub, ln_idx, *, alpha, tm, tf, ride=None):
    mp, d = xp.shape
    ms = xs.shape[0]
    f = w_gate.shape[-1]
    n_p, n_j = mp // tm, f // tf
    prompt_tile = pl.BlockSpec((tm, d), lambda i, j, *_: (i, 0))
    sample_tile = pl.BlockSpec((ms, d), lambda i, j, *_: (0, 0))
    in_specs = [
        prompt_tile,
        sample_tile,
        pl.BlockSpec((None, None, d, tf), lambda i, j, *_: (layer, sub, 0, j)),
        pl.BlockSpec((None, None, d, tf), lambda i, j, *_: (layer, sub, 0, j)),
        pl.BlockSpec((None, None, tf, d), lambda i, j, *_: (layer, sub, j, 0)),
        pl.BlockSpec((None, 1, d), lambda i, j, *_: (ln_idx, 0, 0)),
        pl.BlockSpec((None, 1, d), lambda i, j, *_: (ln_idx, 0, 0)),
    ]
    out_specs = [prompt_tile, sample_tile]
    out_shape = [jax.ShapeDtypeStruct((mp, d), F32), jax.ShapeDtypeStruct((ms, d), F32)]
    operands = [xp, xs, w_gate, w_up, w_down, lng, lnb]
    n_ride = ppb = 0
    prefetch = []
    if ride is not None:
        cache_k, page_table, first_block = ride
        _, page, nh, dh = cache_k.shape
        n_pages = page_table.shape[1]
        ppb = MOBA_BLOCK // page
        bps = RIDE_BLOCKS_PER_STEP
        n_here = min(n_p * n_j * bps, page_table.size // ppb - first_block)
        assert n_here > 0 and n_here % bps == 0
        last_step = n_here // bps - 1
        n_ride = bps * ppb

        def step_of(i, j):
            return jnp.minimum(i * n_j + j, last_step)

        def page_spec(k):
            def index(i, j, pt):
                pg = (first_block + step_of(i, j) * bps) * ppb + k
                return (pt[pg // n_pages, pg % n_pages], 0, 0, 0)
            return pl.BlockSpec((None, page, nh, dh), index)

        in_specs += [page_spec(k) for k in range(n_ride)]
        operands += [cache_k] * n_ride
        out_specs.append(pl.BlockSpec((bps, nh, dh), lambda i, j, pt: (step_of(i, j), 0, 0)))
        out_shape.append(jax.ShapeDtypeStruct((n_here, nh, dh), F32))
        prefetch = [page_table]
    return pl.pallas_call(
        functools.partial(_ffn_body, alpha=alpha, n_j=n_j, n_p=n_p, n_ride=n_ride, pages_per_blk=ppb),
        grid_spec=pltpu.PrefetchScalarGridSpec(
            num_scalar_prefetch=len(prefetch),
            grid=(n_p, n_j),
            in_specs=in_specs,
            out_specs=out_specs,
            scratch_shapes=[pltpu.VMEM((tm, d), BF16), pltpu.VMEM((ms, d), BF16)],
        ),
        out_shape=out_shape,
        compiler_params=_params("arbitrary", "arbitrary"),
        name="ffn",
    )(*prefetch, *operands)


def _conv_body(*refs, alpha, n_j, tiles_per_seq, seq_len, multi_seq):
    if multi_seq:
        (x_ref, wb_ref, wc_ref, wh_ref, wk_ref, wo_ref, g_ref, b_ref, p1_ref, p2_ref,
         o_ref, tail_ref, xb_ref, halo_ref) = refs
    else:
        (x_ref, wb_ref, wc_ref, wh_ref, wk_ref, wo_ref, g_ref, b_ref, prev_ref,
         o_ref, tail_ref, xb_ref, halo_ref) = refs
    i = pl.program_id(0)
    j = pl.program_id(1)
    tm = x_ref.shape[0]
    tail_rows = tail_ref.shape[0]

    _start_tile(x_ref, xb_ref, o_ref, j)
    if not multi_seq:
        @pl.when((i % tiles_per_seq) == 0)
        def _():
            halo_ref[j] = prev_ref[...]

    xb = xb_ref[...]
    gate_b = jnp.dot(xb, wb_ref[...].astype(BF16), preferred_element_type=F32)
    gate_c = jnp.dot(xb, wc_ref[...].astype(BF16), preferred_element_type=F32)
    whb = wh_ref[...].astype(BF16)
    half = tm // 2 if tm % (4 * SUBLANES) == 0 else tm
    hh = jnp.concatenate([jnp.dot(xb[r:r + half], whb, preferred_element_type=F32)
                          for r in range(0, tm, half)], axis=0)
    u = gate_c * hh
    s1 = pltpu.roll(u, 1, axis=0)
    s2 = pltpu.roll(u, 2, axis=0)
    if multi_seq:
        t = lax.broadcasted_iota(jnp.int32, u.shape, 0) % seq_len
        s1 = jnp.where(t >= 1, s1, p1_ref[...])
        s2 = jnp.where(t >= 2, s2, p2_ref[...])
    else:
        halo = halo_ref[j]
        row = lax.broadcasted_iota(jnp.int32, halo.shape, 0)
        before1 = jnp.broadcast_to(halo[SUBLANES - 1:SUBLANES, :], halo.shape)
        before2 = jnp.broadcast_to(halo[SUBLANES - 2:SUBLANES - 1, :], halo.shape)
        top1 = jnp.where(row == 0, before1, s1[0:SUBLANES, :])
        top2 = jnp.where(row == 0, before2, jnp.where(row == 1, before1, s2[0:SUBLANES, :]))
        s1 = jnp.concatenate([top1, s1[SUBLANES:, :]], axis=0)
        s2 = jnp.concatenate([top2, s2[SUBLANES:, :]], axis=0)
    wk = wk_ref[...]
    y = wk[0:1, :] * s2
    y = y + wk[1:2, :] * s1
    y = y + wk[2:3, :] * u
    yb = (gate_b * y).astype(BF16)
    o_ref[...] += jnp.dot(yb, wo_ref[...].astype(BF16), preferred_element_type=F32)

    halo_ref[j] = u[tm - SUBLANES:, :]
    tail_ref[...] = u[tm - tail_rows:, :]

    @pl.when(j == n_j - 1)
    def _():
        _post_norm_rows(x_ref, o_ref, g_ref, b_ref, alpha, 1.0, _row_chunk(tm))


def _conv(x, w_in, w_conv, w_out, lng, lnb, layer, ln_idx, prev, *, alpha, tm, tn, seq_len):
    m, d = x.shape
    n_i, n_j = m // tm, d // tn
    multi_seq = seq_len < tm
    tiles_per_seq = max(seq_len // tm, 1)
    tail_rows = tm if multi_seq else SUBLANES
    in_specs = [
        pl.BlockSpec((tm, d), lambda i, j: (i, 0)),
        pl.BlockSpec((None, d, tn), lambda i, j: (layer, 0, j)),
        pl.BlockSpec((None, d, tn), lambda i, j: (layer, 0, n_j + j)),
        pl.BlockSpec((None, d, tn), lambda i, j: (layer, 0, 2 * n_j + j)),
        pl.BlockSpec((None, CONV_W, tn), lambda i, j: (layer, 0, j)),
        pl.BlockSpec((None, tn, d), lambda i, j: (layer, j, 0)),
        pl.BlockSpec((None, 1, d), lambda i, j: (ln_idx, 0, 0)),
        pl.BlockSpec((None, 1, d), lambda i, j: (ln_idx, 0, 0)),
    ]
    if multi_seq:
        in_specs += [pl.BlockSpec((tm, tn), lambda i, j: (i, j))] * 2
        extra = tuple(prev)
    else:
        in_specs += [pl.BlockSpec((None, SUBLANES, tn), lambda i, j: (i // tiles_per_seq, 0, j))]
        extra = (prev,)
    return pl.pallas_call(
        functools.partial(_conv_body, alpha=alpha, n_j=n_j, tiles_per_seq=tiles_per_seq,
                          seq_len=seq_len, multi_seq=multi_seq),
        grid=(n_i, n_j),
        in_specs=in_specs,
        out_specs=[
            pl.BlockSpec((tm, d), lambda i, j: (i, 0)),
            pl.BlockSpec((None, tail_rows, tn), lambda i, j: (i, 0, j)),
        ],
        out_shape=[
            jax.ShapeDtypeStruct((m, d), F32),
            jax.ShapeDtypeStruct((n_i, tail_rows, d), F32),
        ],
        scratch_shapes=[
            pltpu.VMEM((tm, d), BF16),
            pltpu.VMEM((n_j, SUBLANES, tn), F32),
        ],
        compiler_params=_params("arbitrary", "arbitrary"),
        name="conv",
    )(x, w_in, w_in, w_in, w_conv, w_out, lng, lnb, *extra)


def _rope_cols(y, cos, sin_signed):
    heads = []
    for c in range(y.shape[1] // LANES):
        yh = y[:, c * LANES:(c + 1) * LANES]
        heads.append(yh * cos + pltpu.roll(yh, LANES // 2, axis=1) * sin_signed)
    return heads[0] if len(heads) == 1 else jnp.concatenate(heads, axis=1)


def _proj_body(*refs, rope):
    n_w = len(rope)
    x_ref = refs[0]
    w_refs = refs[1:1 + n_w]
    cos_ref, sin_ref = refs[1 + n_w:3 + n_w]
    o_refs = refs[3 + n_w:3 + 2 * n_w]
    xb_ref = refs[3 + 2 * n_w]
    j = pl.program_id(1)

    @pl.when(j == 0)
    def _():
        _cast_rows(x_ref, xb_ref, _row_chunk(x_ref.shape[0]))

    xb = xb_ref[...]
    for w_ref, o_ref, use_rope in zip(w_refs, o_refs, rope):
        y = jnp.dot(xb, w_ref[...].astype(BF16), preferred_element_type=F32)
        if use_rope:
            y = _rope_cols(y, cos_ref[...], sin_ref[...])
        o_ref[...] = y


def _proj(x, weights, rope, cos, sin_signed, *, tm, tn):
    m, d = x.shape
    n = weights[0][0].shape[-1]
    n_i, n_j = m // tm, n // tn
    pos_tiles = cos.shape[0] // tm
    w_specs = []
    for w, lead in weights:
        if lead is None:
            w_specs.append(pl.BlockSpec((d, tn), lambda i, j: (0, j)))
        else:
            w_specs.append(pl.BlockSpec((None, d, tn), lambda i, j, lead=lead: (lead, 0, j)))
    pos_spec = pl.BlockSpec((tm, LANES), lambda i, j: (i % pos_tiles, 0))
    return pl.pallas_call(
        functools.partial(_proj_body, rope=tuple(rope)),
        grid=(n_i, n_j),
        in_specs=[pl.BlockSpec((tm, d), lambda i, j: (i, 0))] + w_specs + [pos_spec, pos_spec],
        out_specs=[pl.BlockSpec((tm, tn), lambda i, j: (i, j))] * len(weights),
        out_shape=[jax.ShapeDtypeStruct((m, n), F32)] * len(weights),
        scratch_shapes=[pltpu.VMEM((tm, d), BF16)],
        compiler_params=_params("parallel", "arbitrary"),
        name="proj",
    )(x, *[w for w, _ in weights], cos, sin_signed)


def _oproj_body(x_ref, a_ref, w_ref, g_ref, b_ref, o_ref, *, alpha, n_j):
    j = pl.program_id(1)
    _start_tile(x_ref, None, o_ref, j)
    o_ref[...] += jnp.dot(a_ref[...], w_ref[...].astype(BF16), preferred_element_type=F32)

    @pl.when(j == n_j - 1)
    def _():
        _post_norm_rows(x_ref, o_ref, g_ref, b_ref, alpha, 1.0, _row_chunk(x_ref.shape[0]))


def _oproj(x, a, w_o, lng, lnb, layer, ln_idx, *, alpha, tm, tk):
    m, d = x.shape
    hd = a.shape[1]
    n_i, n_j = m // tm, hd // tk
    return pl.pallas_call(
        functools.partial(_oproj_body, alpha=alpha, n_j=n_j),
        grid=(n_i, n_j),
        in_specs=[
            pl.BlockSpec((tm, d), lambda i, j: (i, 0)),
            pl.BlockSpec((tm, tk), lambda i, j: (i, j)),
            pl.BlockSpec((None, tk, d), lambda i, j: (layer, j, 0)),
            pl.BlockSpec((None, 1, d), lambda i, j: (ln_idx, 0, 0)),
            pl.BlockSpec((None, 1, d), lambda i, j: (ln_idx, 0, 0)),
        ],
        out_specs=pl.BlockSpec((tm, d), lambda i, j: (i, 0)),
        out_shape=jax.ShapeDtypeStruct((m, d), F32),
        compiler_params=_params("parallel", "arbitrary"),
        name="oproj",
    )(x, a, w_o, lng, lnb)


def _select_blocks(gate_t, n_valid, topk):
    nb = gate_t.shape[0]
    blk = lax.broadcasted_iota(jnp.int32, gate_t.shape, 0)
    g = jnp.where(blk < n_valid, gate_t, NEG_INF)
    sel = jnp.zeros(gate_t.shape, jnp.bool_)
    for s in range(topk):
        best = jnp.max(g, axis=0, keepdims=True)
        idx = jnp.min(jnp.where(g == best, blk, nb), axis=0, keepdims=True)
        hit = blk == idx
        sel = jnp.logical_or(sel, jnp.logical_and(hit, s < n_valid))
        g = jnp.where(hit, -jnp.inf, g)
    return sel


def _attn_body(q_ref, k_ref, v_ref, o_ref, means_ref, kb_ref, vt_ref, bias_ref, s_ref, p_ref,
               *, scale, hps):
    i = pl.program_id(2)
    blk = MOBA_BLOCK
    nb = k_ref.shape[0] // blk
    dh = LANES

    @pl.when(i == 0)
    def _():
        for hh in range(hps):
            cols = slice(hh * dh, (hh + 1) * dh)
            for n in range(nb):
                rows = slice(n * blk, (n + 1) * blk)
                kf = k_ref[rows, cols]
                means_ref[hh, n:n + 1, :] = jnp.sum(kf, axis=0, keepdims=True) * (1.0 / blk)
                kb_ref[hh, n] = kf.astype(BF16)
                vt_ref[hh, n] = v_ref[rows, cols].T.astype(BF16)

    key = lax.broadcasted_iota(jnp.int32, (blk, blk), 0)
    qry = lax.broadcasted_iota(jnp.int32, (blk, blk), 1)
    qts = [q_ref[:, hh * dh:(hh + 1) * dh].T for hh in range(hps)]
    qbs = [(qt * (scale * LOG2_E)).astype(BF16) for qt in qts]

    def logits(n):
        return [jnp.dot(kb_ref[hh, n], qbs[hh], preferred_element_type=F32) for hh in range(hps)]

    own, first_past = logits(i), logits(0)
    for hh in range(hps):
        gate_t = jnp.dot(means_ref[hh], qts[hh], precision=lax.Precision.HIGHEST,
                         preferred_element_type=F32)
        bias_ref[hh] = jnp.where(_select_blocks(gate_t, i, MOBA_TOPK), 0.0, NEG_INF)
    carry = []
    for hh in range(hps):
        s = jnp.where(key <= qry, own[hh], NEG_INF)
        m = jnp.max(s, axis=0, keepdims=True)
        p = jnp.exp2(s - m)
        p_ref[hh] = p.astype(BF16)
        s_ref[hh] = first_past[hh]
        carry += [m, jnp.sum(p, axis=0, keepdims=True), jnp.zeros((dh, blk), F32),
                  jnp.ones((1, blk), F32)]

    def pv_products(v_blk):
        return [jnp.dot(vt_ref[hh, v_blk], p_ref[hh], preferred_element_type=F32)
                for hh in range(hps)]

    def past(n, carry):
        ss_next = logits(jnp.minimum(n + 1, nb - 1))
        pvs = pv_products(jnp.where(n == 0, i, n - 1))
        out, ps = [], []
        for hh in range(hps):
            m, l, acc, corr_prev = carry[4 * hh:4 * hh + 4]
            s = s_ref[hh]
            bias = bias_ref[hh, pl.ds(n, 1), :]
            m_new = jnp.maximum(m, jnp.max(s, axis=0, keepdims=True) + bias)
            p = jnp.exp2(s + (bias - m_new))
            corr = jnp.exp2(m - m_new)
            ps.append(p.astype(BF16))
            out += [m_new, l * corr + jnp.sum(p, axis=0, keepdims=True),
                    acc * corr_prev + pvs[hh], corr]
        for hh in range(hps):
            p_ref[hh] = ps[hh]
            s_ref[hh] = ss_next[hh]
        return tuple(out)

    carry = lax.fori_loop(0, i, past, tuple(carry))
    pvs = pv_products(jnp.where(i == 0, i, i - 1))
    for hh in range(hps):
        _, l, acc, corr_prev = carry[4 * hh:4 * hh + 4]
        acc = acc * corr_prev + pvs[hh]
        o_ref[:, hh * dh:(hh + 1) * dh] = (acc / l).T.astype(o_ref.dtype)


def _moba_prompt(q, k, v, *, n_heads, hps):
    b, t, hd = q.shape
    dh = hd // n_heads
    nb = t // MOBA_BLOCK
    w = hps * dh
    return pl.pallas_call(
        functools.partial(_attn_body, scale=dh ** -0.5, hps=hps),
        grid=(b, n_heads // hps, nb),
        in_specs=[
            pl.BlockSpec((None, MOBA_BLOCK, w), lambda bi, h, i: (bi, i, h)),
            pl.BlockSpec((None, t, w), lambda bi, h, i: (bi, 0, h)),
            pl.BlockSpec((None, t, w), lambda bi, h, i: (bi, 0, h)),
        ],
        out_specs=pl.BlockSpec((None, MOBA_BLOCK, w), lambda bi, h, i: (bi, i, h)),
        out_shape=jax.ShapeDtypeStruct((b, t, hd), BF16),
        scratch_shapes=[
            pltpu.VMEM((hps, nb, dh), F32),
            pltpu.VMEM((hps, nb, MOBA_BLOCK, dh), BF16),
            pltpu.VMEM((hps, nb, dh, MOBA_BLOCK), BF16),
            pltpu.VMEM((hps, nb, MOBA_BLOCK), F32),
            pltpu.VMEM((hps, MOBA_BLOCK, MOBA_BLOCK), F32),
            pltpu.VMEM((hps, MOBA_BLOCK, MOBA_BLOCK), BF16),
        ],
        compiler_params=_params("parallel", "parallel", "arbitrary"),
        name="moba_prompt",
    )(q, k, v)


def _means_body(pt_ref, *refs, pages_per_blk):
    _block_means(refs[:-1], refs[-1], pages_per_blk)


def _cache_means(cache_k, page_table, *, blocks_per_step):
    _, page, nh, dh = cache_k.shape
    b, n_pages = page_table.shape
    ppb = MOBA_BLOCK // page
    pps = ppb * blocks_per_step
    page_specs = [
        pl.BlockSpec((None, page, nh, dh), lambda bi, s, pt, k=k: (pt[bi, s * pps + k], 0, 0, 0))
        for k in range(pps)
    ]
    return pl.pallas_call(
        functools.partial(_means_body, pages_per_blk=ppb),
        grid_spec=pltpu.PrefetchScalarGridSpec(
            num_scalar_prefetch=1,
            grid=(b, n_pages // pps),
            in_specs=page_specs,
            out_specs=pl.BlockSpec((None, blocks_per_step, nh, dh), lambda bi, s, pt: (bi, s, 0, 0)),
        ),
        out_shape=jax.ShapeDtypeStruct((b, n_pages // ppb, nh, dh), F32),
        compiler_params=_params("parallel", "arbitrary"),
        name="cache_means",
    )(page_table, *([cache_k] * pps))


def _sgate_body(means_ref, q_ref, o_ref, *, topk):
    nq = q_ref.shape[0]
    nb, nh, _ = means_ref.shape
    o_ref[...] = jnp.zeros(o_ref.shape, jnp.int32)
    means = means_ref[...]
    blk = lax.broadcasted_iota(jnp.int32, (nb, nh), 0)
    for qi in range(nq):
        g = jnp.sum(means * q_ref[qi][None], axis=-1)
        for s in range(topk):
            best = jnp.max(g, axis=0, keepdims=True)
            idx = jnp.min(jnp.where(g == best, blk, nb), axis=0, keepdims=True)
            o_ref[qi, s:s + 1, 0:nh] = idx
            g = jnp.where(blk == idx, -jnp.inf, g)


def _sample_select(means, q4, *, topk):
    b, nb, nh, dh = means.shape
    nq = q4.shape[1]
    return pl.pallas_call(
        functools.partial(_sgate_body, topk=topk),
        grid=(b,),
        in_specs=[
            pl.BlockSpec((None, nb, nh, dh), lambda bi: (bi, 0, 0, 0)),
            pl.BlockSpec((None, nq, nh, dh), lambda bi: (bi, 0, 0, 0)),
        ],
        out_specs=pl.BlockSpec((None, nq, SUBLANES, LANES), lambda bi: (bi, 0, 0, 0)),
        out_shape=jax.ShapeDtypeStruct((b, nq, SUBLANES, LANES), jnp.int32),
        compiler_params=_params("parallel"),
        name="sample_select",
    )(means, q4)


def _sattn_body(pt_ref, idx_ref, q_ref, kn_ref, vn_ref, ck_ref, cv_ref, o_ref,
                kg_ref, vg_ref, sem, *, nq, topk, page, scale):
    b = pl.program_id(0)
    h = pl.program_id(1)
    nh = pl.num_programs(1)
    ppb = MOBA_BLOCK // page
    n_slots = nq * topk

    step = b * nh + h
    buf = lax.rem(step, 2)

    def copies(step_b, step_h, dst, slot, half):
        blk = idx_ref[(step_b * nh + step_h) * n_slots + slot]
        pg = pt_ref[step_b, blk * ppb + half]
        rows = pl.ds(slot * MOBA_BLOCK + half * page, page)
        return (pltpu.make_async_copy(ck_ref.at[pg, :, step_h, :], kg_ref.at[dst, rows, :], sem.at[dst, 0]),
                pltpu.make_async_copy(cv_ref.at[pg, :, step_h, :], vg_ref.at[dst, rows, :], sem.at[dst, 1]))

    def for_all_copies(step_b, step_h, dst, fn):
        for slot in range(n_slots):
            for half in range(ppb):
                for queue, cp in enumerate(copies(step_b, step_h, dst, slot, half)):
                    fn(cp, queue)

    @pl.when(step == 0)
    def _():
        for_all_copies(b, h, buf, lambda cp, queue: cp.start(priority=queue))

    @pl.when(step + 1 < pl.num_programs(0) * nh)
    def _():
        wrap = h + 1 == nh
        for_all_copies(jnp.where(wrap, b + 1, b), jnp.where(wrap, 0, h + 1), 1 - buf,
                       lambda cp, queue: cp.start(priority=queue))

    for_all_copies(b, h, buf, lambda cp, queue: cp.wait())

    rows = q_ref.shape[0]
    qb = q_ref[...].astype(BF16)
    n_keys = n_slots * MOBA_BLOCK
    s_past = lax.dot_general(qb, kg_ref[buf].astype(BF16), _NT, preferred_element_type=F32) * scale
    s_new = lax.dot_general(qb, kn_ref[...].astype(BF16), _NT, preferred_element_type=F32) * scale
    r_past = lax.broadcasted_iota(jnp.int32, (rows, n_keys), 0)
    c_past = lax.broadcasted_iota(jnp.int32, (rows, n_keys), 1)
    lo = r_past * (topk * MOBA_BLOCK)
    m_past = jnp.logical_and(c_past >= lo, c_past < lo + topk * MOBA_BLOCK)
    r_new = lax.broadcasted_iota(jnp.int32, (rows, rows), 0)
    c_new = lax.broadcasted_iota(jnp.int32, (rows, rows), 1)
    m_new = jnp.logical_and(c_new <= r_new, c_new < nq)
    s_past = jnp.where(m_past, s_past, NEG_INF)
    s_new = jnp.where(m_new, s_new, NEG_INF)
    top = jnp.maximum(jnp.max(s_past, axis=1, keepdims=True), jnp.max(s_new, axis=1, keepdims=True))
    p_past = jnp.where(m_past, jnp.exp(s_past - top), 0.0)
    p_new = jnp.where(m_new, jnp.exp(s_new - top), 0.0)
    l = jnp.sum(p_past, axis=1, keepdims=True) + jnp.sum(p_new, axis=1, keepdims=True)
    acc = jnp.dot(p_past.astype(BF16), vg_ref[buf].astype(BF16), preferred_element_type=F32)
    acc = acc + jnp.dot(p_new.astype(BF16), vn_ref[...].astype(BF16), preferred_element_type=F32)
    o_ref[...] = (acc / l).astype(o_ref.dtype)


def _moba_sample(q8, k8, v8, cache_k, cache_v, page_table, sel_idx, *, nq, n_heads):
    b, rows, hd = q8.shape
    dh = hd // n_heads
    page = cache_k.shape[1]
    n_keys = nq * MOBA_TOPK * MOBA_BLOCK
    blk3 = pl.BlockSpec((None, rows, dh), lambda bi, h, pt, idx: (bi, 0, h))
    return pl.pallas_call(
        functools.partial(_sattn_body, nq=nq, topk=MOBA_TOPK, page=page, scale=dh ** -0.5),
        grid_spec=pltpu.PrefetchScalarGridSpec(
            num_scalar_prefetch=2,
            grid=(b, n_heads),
            in_specs=[blk3, blk3, blk3,
                      pl.BlockSpec(memory_space=pl.ANY), pl.BlockSpec(memory_space=pl.ANY)],
            out_specs=blk3,
            scratch_shapes=[
                pltpu.VMEM((2, n_keys, dh), F32),
                pltpu.VMEM((2, n_keys, dh), F32),
                pltpu.SemaphoreType.DMA((2, 2)),
            ],
        ),
        out_shape=jax.ShapeDtypeStruct((b, rows, hd), BF16),
        compiler_params=_params("arbitrary", "arbitrary"),
        name="moba_sample",
    )(page_table, sel_idx, q8, k8, v8, cache_k, cache_v)


def _rope_tables(pos, dh):
    half = dh // 2
    inv = ROPE_THETA ** (-2.0 * jnp.arange(half, dtype=F32) / dh)
    ang = pos.astype(F32)[:, None] * inv[None, :]
    cos, sin = jnp.cos(ang), jnp.sin(ang)
    return jnp.concatenate([cos, cos], axis=1), jnp.concatenate([-sin, sin], axis=1)


def _pad_rows(a, rows):
    return jnp.pad(a, ((0, 0), (0, rows - a.shape[1]), (0, 0)))


def _trunk(x, n_seq, seq_len, pos, conv_prev, past, w, *, tm, tf, tn):
    (lng, lnb, w_ffn_gate, w_ffn_up, w_ffn_down, w_conv_in, w_conv, w_conv_out,
     w_k, w_v, w_q, w_o) = w
    depth = w_ffn_gate.shape[0]
    n_a = w_conv_in.shape[0]
    alpha = float((2 * depth) ** 0.25)
    m, d = x.shape
    n_heads = w_k.shape[1] // LANES
    dh = w_k.shape[1] // n_heads
    cos, sin_signed = _rope_tables(pos, dh)
    if past is not None:
        cos, sin_signed = jnp.tile(cos, (n_seq, 1)), jnp.tile(sin_signed, (n_seq, 1))
    wide = 2 * tn if w_k.shape[1] % (2 * tn) == 0 else tn
    wider = 2 * wide if w_k.shape[1] % (2 * wide) == 0 else wide
    conv_states = []
    for layer in range(depth):
        if layer == n_a:
            k_new, v_new = _proj(x, [(w_k, None), (w_v, None)], (True, False), cos, sin_signed,
                                 tm=tm, tn=wide)
            if past is not None:
                means = past[3]()
        x = yield x, layer, 0, 3 * layer
        if layer < n_a:
            if seq_len < tm:
                prev = conv_prev[layer]
                p1 = jnp.zeros((n_seq, seq_len, d), F32).at[:, 0].set(prev[:, 1])
                p2 = jnp.zeros((n_seq, seq_len, d), F32).at[:, 0].set(prev[:, 0]).at[:, 1].set(prev[:, 1])
                prev_arg = (p1.reshape(m, d), p2.reshape(m, d))
            else:
                prev_arg = jnp.pad(conv_prev[layer], ((0, 0), (SUBLANES - (CONV_W - 1), 0), (0, 0)))
            x, tail = _conv(x, w_conv_in, w_conv, w_conv_out, lng, lnb, layer, 3 * layer + 1,
                            prev_arg, alpha=alpha, tm=tm, tn=tn, seq_len=seq_len)
            if seq_len < tm:
                st = tail.reshape(n_seq, seq_len, d)[:, seq_len - (CONV_W - 1):]
            else:
                tps = seq_len // tm
                st = tail[tps - 1::tps, SUBLANES - (CONV_W - 1):]
            conv_states.append(st)
        else:
            jb = layer - n_a
            (q,) = _proj(x, [(w_q, jb)], (True,), cos, sin_signed, tm=tm, tn=wider)
            if past is None:
                a = _moba_prompt(q.reshape(n_seq, seq_len, -1), k_new.reshape(n_seq, seq_len, -1),
                                 v_new.reshape(n_seq, seq_len, -1), n_heads=n_heads,
                                 hps=4 if n_heads % 4 == 0 else 1)
                a = a.reshape(m, -1)
            else:
                q3 = q.reshape(n_seq, seq_len, -1)
                sel = _sample_select(means, q3.reshape(n_seq, seq_len, n_heads, dh), topk=MOBA_TOPK)
                sel = sel[:, :, :MOBA_TOPK, :n_heads].transpose(0, 3, 1, 2).reshape(-1)
                a8 = _moba_sample(_pad_rows(q3, SUBLANES),
                                  _pad_rows(k_new.reshape(n_seq, seq_len, -1), SUBLANES),
                                  _pad_rows(v_new.reshape(n_seq, seq_len, -1), SUBLANES),
                                  past[0], past[1], past[2], sel, nq=seq_len, n_heads=n_heads)
                a = a8[:, :seq_len].reshape(m, -1)
            x = _oproj(x, a, w_o, lng, lnb, jb, 3 * layer + 1, alpha=alpha, tm=tm, tk=wider)
        x = yield x, layer, 1, 3 * layer + 2
    return x, jnp.stack(conv_states), k_new, v_new


def kernel(x_prompt, x_sample, state_conv, cache_k, cache_v, page_table, ln_g, ln_b, w_ffn_gate, w_ffn_up, w_ffn_down, w_conv_in, w_conv, w_conv_out, w_k, w_v, w_q, w_o):
    bp, tp, d = x_prompt.shape
    bs, ts, _ = x_sample.shape
    n_heads, dh = cache_k.shape[2], cache_k.shape[3]
    n_a = w_conv_in.shape[0]
    past_len = page_table.shape[1] * cache_k.shape[1]
    assert dh == LANES and w_conv.shape[1] == CONV_W
    assert tp % MOBA_BLOCK == 0 and past_len % MOBA_BLOCK == 0 and MOBA_BLOCK % cache_k.shape[1] == 0
    assert CONV_W - 1 <= ts <= SUBLANES and past_len // MOBA_BLOCK >= MOBA_TOPK
    lng = ln_g.reshape(-1, 1, d)
    lnb = ln_b.reshape(-1, 1, d)
    w = (lng, lnb, w_ffn_gate, w_ffn_up, w_ffn_down, w_conv_in, w_conv, w_conv_out, w_k, w_v, w_q, w_o)
    f = w_ffn_gate.shape[-1]
    tf = 256 if f % 256 == 0 else f
    tn = 256 if d % 256 == 0 else d

    conv0 = jnp.zeros((n_a, bp, CONV_W - 1, d), F32)
    tm_p = min(1024, tp)
    pos_s = past_len + jnp.arange(ts, dtype=jnp.int32)
    prompt = _trunk(x_prompt.reshape(bp * tp, d), bp, tp, jnp.arange(tp, dtype=jnp.int32),
                    conv0, None, w, tm=tm_p, tf=tf, tn=tn)
    n_blocks = past_len // MOBA_BLOCK
    mean_parts = []

    def block_means():
        if sum(part.shape[0] for part in mean_parts) == bs * n_blocks:
            return jnp.concatenate(mean_parts, axis=0).reshape(bs, n_blocks, n_heads, dh)
        return _cache_means(cache_k, page_table, blocks_per_step=2 if n_blocks % 2 == 0 else 1)

    sample = _trunk(x_sample.reshape(bs * ts, d), bs, ts, pos_s, state_conv,
                    (cache_k, cache_v, page_table, block_means), w, tm=bs * ts, tf=tf, tn=tn)
    alpha = float((2 * w_ffn_gate.shape[0]) ** 0.25)
    req_p, req_s = next(prompt), next(sample)
    results = None
    while results is None:
        (xp, layer, sub, ln_idx), xs = req_p, req_s[0]
        first_block = sum(part.shape[0] for part in mean_parts)
        ride = (cache_k, page_table, first_block) if first_block < bs * n_blocks else None
        yp, ys, *part = _ffn(xp, xs, w_ffn_gate, w_ffn_up, w_ffn_down, lng, lnb, layer, sub, ln_idx,
                             alpha=alpha, tm=tm_p, tf=tf, ride=ride)
        mean_parts += part
        try:
            req_p, req_s = prompt.send(yp), sample.send(ys)
        except StopIteration as done:
            try:
                sample.send(ys)
            except StopIteration as done_s:
                results = done.value, done_s.value
    (y_p, conv_p, k_p, v_p), (y_s, conv_s, k_s, v_s) = results
    return (y_p.reshape(bp, tp, d), y_s.reshape(bs, ts, d), conv_p, conv_s,
            k_p.reshape(bp, tp, n_heads, dh), v_p.reshape(bp, tp, n_heads, dh),
            k_s.reshape(bs, ts, n_heads, dh), v_s.reshape(bs, ts, n_heads, dh))
```

```python
import functools

import jax
import jax.numpy as jnp
from jax import lax
from jax.experimental import pallas as pl
from jax.experimental.pallas import tpu as pltpu

F32 = jnp.float32
BF16 = jnp.bfloat16

MOBA_BLOCK = 256
MOBA_TOPK = 3
CONV_W = 3
ROPE_THETA = 10000.0
LN_EPS = 1e-5
NEG_INF = -1e30
LOG2_E = 1.4426950408889634

SUBLANES = 8
LANES = 128
VMEM_LIMIT_BYTES = 60 * 1024 * 1024

_NT = (((1,), (1,)), ((), ()))


def _params(*sem):
    return pltpu.CompilerParams(dimension_semantics=sem, vmem_limit_bytes=VMEM_LIMIT_BYTES)


def _layer_norm(z, g, b):
    mu = jnp.mean(z, axis=-1, keepdims=True)
    d = z - mu
    var = jnp.mean(d * d, axis=-1, keepdims=True)
    return d * lax.rsqrt(var + LN_EPS) * g + b


def _cast_rows(src_ref, dst_ref, rows):
    n = src_ref.shape[0] // rows

    def body(c, carry):
        r = pl.ds(pl.multiple_of(c * rows, rows), rows)
        dst_ref[r, :] = src_ref[r, :].astype(dst_ref.dtype)
        return carry

    lax.fori_loop(0, n, body, 0)


def _start_tile(x_ref, xb_ref, o_ref, j):
    @pl.when(j == 0)
    def _():
        if xb_ref is not None:
            _cast_rows(x_ref, xb_ref, _row_chunk(x_ref.shape[0]))
        o_ref[...] = jnp.zeros(o_ref.shape, o_ref.dtype)


def _post_norm_rows(x_ref, o_ref, g_ref, b_ref, alpha, scale, rows):
    n = o_ref.shape[0] // rows
    g = g_ref[...]
    b = b_ref[...]

    def body(c, carry):
        r = pl.ds(pl.multiple_of(c * rows, rows), rows)
        z = alpha * x_ref[r, :] + scale * o_ref[r, :]
        o_ref[r, :] = _layer_norm(z, g, b)
        return carry

    lax.fori_loop(0, n, body, 0)


def _row_chunk(tm):
    return min(tm, 256)


def _block_means(page_refs, means_ref, pages_per_blk):
    for blk in range(len(page_refs) // pages_per_blk):
        total = jnp.sum(page_refs[blk * pages_per_blk][...], axis=0)
        for p in range(1, pages_per_blk):
            total = total + jnp.sum(page_refs[blk * pages_per_blk + p][...], axis=0)
        means_ref[blk] = total * (1.0 / MOBA_BLOCK)


def _ffn_tile(x_ref, wg_ref, wu_ref, wd_ref, g_ref, b_ref, o_ref, xb_ref, j, side_work,
              *, alpha, n_j):
    _start_tile(x_ref, xb_ref, o_ref, j)
    xb = xb_ref[...]
    gate = jnp.dot(xb, wg_ref[...].astype(BF16), preferred_element_type=F32)
    up = jnp.dot(xb, wu_ref[...].astype(BF16), preferred_element_type=F32)
    side_work()
    h = (jax.nn.silu(gate) * up).astype(BF16)
    o_ref[...] += jnp.dot(h, wd_ref[...].astype(BF16), preferred_element_type=F32)

    @pl.when(j == n_j - 1)
    def _():
        _post_norm_rows(x_ref, o_ref, g_ref, b_ref, alpha, 0.5, _row_chunk(x_ref.shape[0]))


def _ffn_body(*refs, n_p, n_ride, pages_per_blk, **kw):
    if n_ride:
        refs = refs[1:]
    xp_ref, xs_ref, wg_ref, wu_ref, wd_ref, g_ref, b_ref = refs[:7]
    page_refs = refs[7:7 + n_ride]
    op_ref, os_ref = refs[7 + n_ride:9 + n_ride]
    means_ref = refs[9 + n_ride] if n_ride else None
    xbp_ref, xbs_ref = refs[-2:]
    i = pl.program_id(0)
    j = pl.program_id(1)

    def side_work():
        if n_ride:
            _block_means(page_refs, means_ref, pages_per_blk)

    _ffn_tile(xp_ref, wg_ref, wu_ref, wd_ref, g_ref, b_ref, op_ref, xbp_ref, j, side_work, **kw)

    @pl.when(i == n_p - 1)
    def _():
        _ffn_tile(xs_ref, wg_ref, wu_ref, wd_ref, g_ref, b_ref, os_ref, xbs_ref, j, lambda: None, **kw)


RIDE_BLOCKS_PER_STEP = 1


def _ffn(xp, xs, w_gate, w_up, w_down, lng, lnb, layer, sub, ln_idx, *, alpha, tm, tf, ride=None):
    mp, d = xp.shape
    ms = xs.shape[0]
    f = w_gate.shape[-1]
    n_p, n_j = mp // tm, f // tf
    prompt_tile = pl.BlockSpec((tm, d), lambda i, j, *_: (i, 0))
    sample_tile = pl.BlockSpec((ms, d), lambda i, j, *_: (0, 0))
    in_specs = [
        prompt_tile,
        sample_tile,
        pl.BlockSpec((None, None, d, tf), lambda i, j, *_: (layer, sub, 0, j)),
        pl.BlockSpec((None, None, d, tf), lambda i, j, *_: (layer, sub, 0, j)),
        pl.BlockSpec((None, None, tf, d), lambda i, j, *_: (layer, sub, j, 0)),
        pl.BlockSpec((None, 1, d), lambda i, j, *_: (ln_idx, 0, 0)),
        pl.BlockSpec((None, 1, d), lambda i, j, *_: (ln_idx, 0, 0)),
    ]
    out_specs = [prompt_tile, sample_tile]
    out_shape = [jax.ShapeDtypeStruct((mp, d), F32), jax.ShapeDtypeStruct((ms, d), F32)]
    operands = [xp, xs, w_gate, w_up, w_down, lng, lnb]
    n_ride = ppb = 0
    prefetch = []
    if ride is not None:
        cache_k, page_table, first_block = ride
        _, page, nh, dh = cache_k.shape
        n_pages = page_table.shape[1]
        ppb = MOBA_BLOCK // page
        bps = RIDE_BLOCKS_PER_STEP
        n_here = min(n_p * n_j * bps, page_table.size // ppb - first_block)
        assert n_here > 0 and n_here % bps == 0
        last_step = n_here // bps - 1
        n_ride = bps * ppb

        def step_of(i, j):
            return jnp.minimum(i * n_j + j, last_step)

        def page_spec(k):
            def index(i, j, pt):
                pg = (first_block + step_of(i, j) * bps) * ppb + k
                return (pt[pg // n_pages, pg % n_pages], 0, 0, 0)
            return pl.BlockSpec((None, page, nh, dh), index)

        in_specs += [page_spec(k) for k in range(n_ride)]
        operands += [cache_k] * n_ride
        out_specs.append(pl.BlockSpec((bps, nh, dh), lambda i, j, pt: (step_of(i, j), 0, 0)))
        out_shape.append(jax.ShapeDtypeStruct((n_here, nh, dh), F32))
        prefetch = [page_table]
    return pl.pallas_call(
        functools.partial(_ffn_body, alpha=alpha, n_j=n_j, n_p=n_p, n_ride=n_ride, pages_per_blk=ppb),
        grid_spec=pltpu.PrefetchScalarGridSpec(
            num_scalar_prefetch=len(prefetch),
            grid=(n_p, n_j),
            in_specs=in_specs,
            out_specs=out_specs,
            scratch_shapes=[pltpu.VMEM((tm, d), BF16), pltpu.VMEM((ms, d), BF16)],
        ),
        out_shape=out_shape,
        compiler_params=_params("arbitrary", "arbitrary"),
        name="ffn",
    )(*prefetch, *operands)


def _conv_body(*refs, alpha, n_j, tiles_per_seq, seq_len, multi_seq):
    if multi_seq:
        (x_ref, wb_ref, wc_ref, wh_ref, wk_ref, wo_ref, g_ref, b_ref, p1_ref, p2_ref,
         o_ref, tail_ref, xb_ref, halo_ref) = refs
    else:
        (x_ref, wb_ref, wc_ref, wh_ref, wk_ref, wo_ref, g_ref, b_ref, prev_ref,
         o_ref, tail_ref, xb_ref, halo_ref) = refs
    i = pl.program_id(0)
    j = pl.program_id(1)
    tm = x_ref.shape[0]
    tail_rows = tail_ref.shape[0]

    _start_tile(x_ref, xb_ref, o_ref, j)
    if not multi_seq:
        @pl.when((i % tiles_per_seq) == 0)
        def _():
            halo_ref[j] = prev_ref[...]

    xb = xb_ref[...]
    gate_b = jnp.dot(xb, wb_ref[...].astype(BF16), preferred_element_type=F32)
    gate_c = jnp.dot(xb, wc_ref[...].astype(BF16), preferred_element_type=F32)
    whb = wh_ref[...].astype(BF16)
    half = tm // 2 if tm % (4 * SUBLANES) == 0 else tm
    hh = jnp.concatenate([jnp.dot(xb[r:r + half], whb, preferred_element_type=F32)
                          for r in range(0, tm, half)], axis=0)
    u = gate_c * hh
    s1 = pltpu.roll(u, 1, axis=0)
    s2 = pltpu.roll(u, 2, axis=0)
    if multi_seq:
        t = lax.broadcasted_iota(jnp.int32, u.shape, 0) % seq_len
        s1 = jnp.where(t >= 1, s1, p1_ref[...])
        s2 = jnp.where(t >= 2, s2, p2_ref[...])
    else:
        halo = halo_ref[j]
        row = lax.broadcasted_iota(jnp.int32, halo.shape, 0)
        before1 = jnp.broadcast_to(halo[SUBLANES - 1:SUBLANES, :], halo.shape)
        before2 = jnp.broadcast_to(halo[SUBLANES - 2:SUBLANES - 1, :], halo.shape)
        top1 = jnp.where(row == 0, before1, s1[0:SUBLANES, :])
        top2 = jnp.where(row == 0, before2, jnp.where(row == 1, before1, s2[0:SUBLANES, :]))
        s1 = jnp.concatenate([top1, s1[SUBLANES:, :]], axis=0)
        s2 = jnp.concatenate([top2, s2[SUBLANES:, :]], axis=0)
    wk = wk_ref[...]
    y = wk[0:1, :] * s2
    y = y + wk[1:2, :] * s1
    y = y + wk[2:3, :] * u
    yb = (gate_b * y).astype(BF16)
    o_ref[...] += jnp.dot(yb, wo_ref[...].astype(BF16), preferred_element_type=F32)

    halo_ref[j] = u[tm - SUBLANES:, :]
    tail_ref[...] = u[tm - tail_rows:, :]

    @pl.when(j == n_j - 1)
    def _():
        _post_norm_rows(x_ref, o_ref, g_ref, b_ref, alpha, 1.0, _row_chunk(tm))


def _conv(x, w_in, w_conv, w_out, lng, lnb, layer, ln_idx, prev, *, alpha, tm, tn, seq_len):
    m, d = x.shape
    n_i, n_j = m // tm, d // tn
    multi_seq = seq_len < tm
    tiles_per_seq = max(seq_len // tm, 1)
    tail_rows = tm if multi_seq else SUBLANES
    in_specs = [
        pl.BlockSpec((tm, d), lambda i, j: (i, 0)),
        pl.BlockSpec((None, d, tn), lambda i, j: (layer, 0, j)),
        pl.BlockSpec((None, d, tn), lambda i, j: (layer, 0, n_j + j)),
        pl.BlockSpec((None, d, tn), lambda i, j: (layer, 0, 2 * n_j + j)),
        pl.BlockSpec((None, CONV_W, tn), lambda i, j: (layer, 0, j)),
        pl.BlockSpec((None, tn, d), lambda i, j: (layer, j, 0)),
        pl.BlockSpec((None, 1, d), lambda i, j: (ln_idx, 0, 0)),
        pl.BlockSpec((None, 1, d), lambda i, j: (ln_idx, 0, 0)),
    ]
    if multi_seq:
        in_specs += [pl.BlockSpec((tm, tn), lambda i, j: (i, j))] * 2
        extra = tuple(prev)
    else:
        in_specs += [pl.BlockSpec((None, SUBLANES, tn), lambda i, j: (i // tiles_per_seq, 0, j))]
        extra = (prev,)
    return pl.pallas_call(
        functools.partial(_conv_body, alpha=alpha, n_j=n_j, tiles_per_seq=tiles_per_seq,
                          seq_len=seq_len, multi_seq=multi_seq),
        grid=(n_i, n_j),
        in_specs=in_specs,
        out_specs=[
            pl.BlockSpec((tm, d), lambda i, j: (i, 0)),
            pl.BlockSpec((None, tail_rows, tn), lambda i, j: (i, 0, j)),
        ],
        out_shape=[
            jax.ShapeDtypeStruct((m, d), F32),
            jax.ShapeDtypeStruct((n_i, tail_rows, d), F32),
        ],
        scratch_shapes=[
            pltpu.VMEM((tm, d), BF16),
            pltpu.VMEM((n_j, SUBLANES, tn), F32),
        ],
        compiler_params=_params("arbitrary", "arbitrary"),
        name="conv",
    )(x, w_in, w_in, w_in, w_conv, w_out, lng, lnb, *extra)


def _rope_cols(y, cos, sin_signed):
    heads = []
    for c in range(y.shape[1] // LANES):
        yh = y[:, c * LANES:(c + 1) * LANES]
        heads.append(yh * cos + pltpu.roll(yh, LANES // 2, axis=1) * sin_signed)
    return heads[0] if len(heads) == 1 else jnp.concatenate(heads, axis=1)


def _proj_body(*refs, rope):
    n_w = len(rope)
    x_ref = refs[0]
    w_refs = refs[1:1 + n_w]
    cos_ref, sin_ref = refs[1 + n_w:3 + n_w]
    o_refs = refs[3 + n_w:3 + 2 * n_w]
    xb_ref = refs[3 + 2 * n_w]
    j = pl.program_id(1)

    @pl.when(j == 0)
    def _():
        _cast_rows(x_ref, xb_ref, _row_chunk(x_ref.shape[0]))

    xb = xb_ref[...]
    for w_ref, o_ref, use_rope in zip(w_refs, o_refs, rope):
        y = jnp.dot(xb, w_ref[...].astype(BF16), preferred_element_type=F32)
        if use_rope:
            y = _rope_cols(y, cos_ref[...], sin_ref[...])
        o_ref[...] = y


def _proj(x, weights, rope, cos, sin_signed, *, tm, tn):
    m, d = x.shape
    n = weights[0][0].shape[-1]
    n_i, n_j = m // tm, n // tn
    pos_tiles = cos.shape[0] // tm
    w_specs = []
    for w, lead in weights:
        if lead is None:
            w_specs.append(pl.BlockSpec((d, tn), lambda i, j: (0, j)))
        else:
            w_specs.append(pl.BlockSpec((None, d, tn), lambda i, j, lead=lead: (lead, 0, j)))
    pos_spec = pl.BlockSpec((tm, LANES), lambda i, j: (i % pos_tiles, 0))
    return pl.pallas_call(
        functools.partial(_proj_body, rope=tuple(rope)),
        grid=(n_i, n_j),
        in_specs=[pl.BlockSpec((tm, d), lambda i, j: (i, 0))] + w_specs + [pos_spec, pos_spec],
        out_specs=[pl.BlockSpec((tm, tn), lambda i, j: (i, j))] * len(weights),
        out_shape=[jax.ShapeDtypeStruct((m, n), F32)] * len(weights),
        scratch_shapes=[pltpu.VMEM((tm, d), BF16)],
        compiler_params=_params("parallel", "arbitrary"),
        name="proj",
    )(x, *[w for w, _ in weights], cos, sin_signed)


def _oproj_body(x_ref, a_ref, w_ref, g_ref, b_ref, o_ref, *, alpha, n_j):
    j = pl.program_id(1)
    _start_tile(x_ref, None, o_ref, j)
    o_ref[...] += jnp.dot(a_ref[...], w_ref[...].astype(BF16), preferred_element_type=F32)

    @pl.when(j == n_j - 1)
    def _():
        _post_norm_rows(x_ref, o_ref, g_ref, b_ref, alpha, 1.0, _row_chunk(x_ref.shape[0]))


def _oproj(x, a, w_o, lng, lnb, layer, ln_idx, *, alpha, tm, tk):
    m, d = x.shape
    hd = a.shape[1]
    n_i, n_j = m // tm, hd // tk
    return pl.pallas_call(
        functools.partial(_oproj_body, alpha=alpha, n_j=n_j),
        grid=(n_i, n_j),
        in_specs=[
            pl.BlockSpec((tm, d), lambda i, j: (i, 0)),
            pl.BlockSpec((tm, tk), lambda i, j: (i, j)),
            pl.BlockSpec((None, tk, d), lambda i, j: (layer, j, 0)),
            pl.BlockSpec((None, 1, d), lambda i, j: (ln_idx, 0, 0)),
            pl.BlockSpec((None, 1, d), lambda i, j: (ln_idx, 0, 0)),
        ],
        out_specs=pl.BlockSpec((tm, d), lambda i, j: (i, 0)),
        out_shape=jax.ShapeDtypeStruct((m, d), F32),
        compiler_params=_params("parallel", "arbitrary"),
        name="oproj",
    )(x, a, w_o, lng, lnb)


def _select_blocks(gate_t, n_valid, topk):
    nb = gate_t.shape[0]
    blk = lax.broadcasted_iota(jnp.int32, gate_t.shape, 0)
    g = jnp.where(blk < n_valid, gate_t, NEG_INF)
    sel = jnp.zeros(gate_t.shape, jnp.bool_)
    for s in range(topk):
        best = jnp.max(g, axis=0, keepdims=True)
        idx = jnp.min(jnp.where(g == best, blk, nb), axis=0, keepdims=True)
        hit = blk == idx
        sel = jnp.logical_or(sel, jnp.logical_and(hit, s < n_valid))
        g = jnp.where(hit, -jnp.inf, g)
    return sel


def _attn_body(q_ref, k_ref, v_ref, o_ref, means_ref, kb_ref, vt_ref, bias_ref, s_ref, p_ref,
               *, scale, hps):
    i = pl.program_id(2)
    blk = MOBA_BLOCK
    nb = k_ref.shape[0] // blk
    dh = LANES

    @pl.when(i == 0)
    def _():
        for hh in range(hps):
            cols = slice(hh * dh, (hh + 1) * dh)
            for n in range(nb):
                rows = slice(n * blk, (n + 1) * blk)
                kf = k_ref[rows, cols]
                means_ref[hh, n:n + 1, :] = jnp.sum(kf, axis=0, keepdims=True) * (1.0 / blk)
                kb_ref[hh, n] = kf.astype(BF16)
                vt_ref[hh, n] = v_ref[rows, cols].T.astype(BF16)

    key = lax.broadcasted_iota(jnp.int32, (blk, blk), 0)
    qry = lax.broadcasted_iota(jnp.int32, (blk, blk), 1)
    qts = [q_ref[:, hh * dh:(hh + 1) * dh].T for hh in range(hps)]
    qbs = [(qt * (scale * LOG2_E)).astype(BF16) for qt in qts]

    def logits(n):
        return [jnp.dot(kb_ref[hh, n], qbs[hh], preferred_element_type=F32) for hh in range(hps)]

    own, first_past = logits(i), logits(0)
    for hh in range(hps):
        gate_t = jnp.dot(means_ref[hh], qts[hh], precision=lax.Precision.HIGHEST,
                         preferred_element_type=F32)
        bias_ref[hh] = jnp.where(_select_blocks(gate_t, i, MOBA_TOPK), 0.0, NEG_INF)
    carry = []
    for hh in range(hps):
        s = jnp.where(key <= qry, own[hh], NEG_INF)
        m = jnp.max(s, axis=0, keepdims=True)
        p = jnp.exp2(s - m)
        p_ref[hh] = p.astype(BF16)
        s_ref[hh] = first_past[hh]
        carry += [m, jnp.sum(p, axis=0, keepdims=True), jnp.zeros((dh, blk), F32),
                  jnp.ones((1, blk), F32)]

    def pv_products(v_blk):
        return [jnp.dot(vt_ref[hh, v_blk], p_ref[hh], preferred_element_type=F32)
                for hh in range(hps)]

    def past(n, carry):
        ss_next = logits(jnp.minimum(n + 1, nb - 1))
        pvs = pv_products(jnp.where(n == 0, i, n - 1))
        out, ps = [], []
        for hh in range(hps):
            m, l, acc, corr_prev = carry[4 * hh:4 * hh + 4]
            s = s_ref[hh]
            bias = bias_ref[hh, pl.ds(n, 1), :]
            m_new = jnp.maximum(m, jnp.max(s, axis=0, keepdims=True) + bias)
            p = jnp.exp2(s + (bias - m_new))
            corr = jnp.exp2(m - m_new)
            ps.append(p.astype(BF16))
            out += [m_new, l * corr + jnp.sum(p, axis=0, keepdims=True),
                    acc * corr_prev + pvs[hh], corr]
        for hh in range(hps):
            p_ref[hh] = ps[hh]
            s_ref[hh] = ss_next[hh]
        return tuple(out)

    carry = lax.fori_loop(0, i, past, tuple(carry))
    pvs = pv_products(jnp.where(i == 0, i, i - 1))
    for hh in range(hps):
        _, l, acc, corr_prev = carry[4 * hh:4 * hh + 4]
        acc = acc * corr_prev + pvs[hh]
        o_ref[:, hh * dh:(hh + 1) * dh] = (acc / l).T.astype(o_ref.dtype)


def _moba_prompt(q, k, v, *, n_heads, hps):
    b, t, hd = q.shape
    dh = hd // n_heads
    nb = t // MOBA_BLOCK
    w = hps * dh
    return pl.pallas_call(
        functools.partial(_attn_body, scale=dh ** -0.5, hps=hps),
        grid=(b, n_heads // hps, nb),
        in_specs=[
            pl.BlockSpec((None, MOBA_BLOCK, w), lambda bi, h, i: (bi, i, h)),
            pl.BlockSpec((None, t, w), lambda bi, h, i: (bi, 0, h)),
            pl.BlockSpec((None, t, w), lambda bi, h, i: (bi, 0, h)),
        ],
        out_specs=pl.BlockSpec((None, MOBA_BLOCK, w), lambda bi, h, i: (bi, i, h)),
        out_shape=jax.ShapeDtypeStruct((b, t, hd), BF16),
        scratch_shapes=[
            pltpu.VMEM((hps, nb, dh), F32),
            pltpu.VMEM((hps, nb, MOBA_BLOCK, dh), BF16),
            pltpu.VMEM((hps, nb, dh, MOBA_BLOCK), BF16),
            pltpu.VMEM((hps, nb, MOBA_BLOCK), F32),
            pltpu.VMEM((hps, MOBA_BLOCK, MOBA_BLOCK), F32),
            pltpu.VMEM((hps, MOBA_BLOCK, MOBA_BLOCK), BF16),
        ],
        compiler_params=_params("parallel", "parallel", "arbitrary"),
        name="moba_prompt",
    )(q, k, v)


def _means_body(pt_ref, *refs, pages_per_blk):
    _block_means(refs[:-1], refs[-1], pages_per_blk)


def _cache_means(cache_k, page_table, *, blocks_per_step):
    _, page, nh, dh = cache_k.shape
    b, n_pages = page_table.shape
    ppb = MOBA_BLOCK // page
    pps = ppb * blocks_per_step
    page_specs = [
        pl.BlockSpec((None, page, nh, dh), lambda bi, s, pt, k=k: (pt[bi, s * pps + k], 0, 0, 0))
        for k in range(pps)
    ]
    return pl.pallas_call(
        functools.partial(_means_body, pages_per_blk=ppb),
        grid_spec=pltpu.PrefetchScalarGridSpec(
            num_scalar_prefetch=1,
            grid=(b, n_pages // pps),
            in_specs=page_specs,
            out_specs=pl.BlockSpec((None, blocks_per_step, nh, dh), lambda bi, s, pt: (bi, s, 0, 0)),
        ),
        out_shape=jax.ShapeDtypeStruct((b, n_pages // ppb, nh, dh), F32),
        compiler_params=_params("parallel", "arbitrary"),
        name="cache_means",
    )(page_table, *([cache_k] * pps))


def _sgate_body(means_ref, q_ref, o_ref, *, topk):
    nq = q_ref.shape[0]
    nb, nh, _ = means_ref.shape
    o_ref[...] = jnp.zeros(o_ref.shape, jnp.int32)
    means = means_ref[...]
    blk = lax.broadcasted_iota(jnp.int32, (nb, nh), 0)
    for qi in range(nq):
        g = jnp.sum(means * q_ref[qi][None], axis=-1)
        for s in range(topk):
            best = jnp.max(g, axis=0, keepdims=True)
            idx = jnp.min(jnp.where(g == best, blk, nb), axis=0, keepdims=True)
            o_ref[qi, s:s + 1, 0:nh] = idx
            g = jnp.where(blk == idx, -jnp.inf, g)


def _sample_select(means, q4, *, topk):
    b, nb, nh, dh = means.shape
    nq = q4.shape[1]
    return pl.pallas_call(
        functools.partial(_sgate_body, topk=topk),
        grid=(b,),
        in_specs=[
            pl.BlockSpec((None, nb, nh, dh), lambda bi: (bi, 0, 0, 0)),
            pl.BlockSpec((None, nq, nh, dh), lambda bi: (bi, 0, 0, 0)),
        ],
        out_specs=pl.BlockSpec((None, nq, SUBLANES, LANES), lambda bi: (bi, 0, 0, 0)),
        out_shape=jax.ShapeDtypeStruct((b, nq, SUBLANES, LANES), jnp.int32),
        compiler_params=_params("parallel"),
        name="sample_select",
    )(means, q4)


def _sattn_body(pt_ref, idx_ref, q_ref, kn_ref, vn_ref, ck_ref, cv_ref, o_ref,
                kg_ref, vg_ref, sem, *, nq, topk, page, scale):
    b = pl.program_id(0)
    h = pl.program_id(1)
    nh = pl.num_programs(1)
    ppb = MOBA_BLOCK // page
    n_slots = nq * topk

    step = b * nh + h
    buf = lax.rem(step, 2)

    def copies(step_b, step_h, dst, slot, half):
        blk = idx_ref[(step_b * nh + step_h) * n_slots + slot]
        pg = pt_ref[step_b, blk * ppb + half]
        rows = pl.ds(slot * MOBA_BLOCK + half * page, page)
        return (pltpu.make_async_copy(ck_ref.at[pg, :, step_h, :], kg_ref.at[dst, rows, :], sem.at[dst, 0]),
                pltpu.make_async_copy(cv_ref.at[pg, :, step_h, :], vg_ref.at[dst, rows, :], sem.at[dst, 1]))

    def for_all_copies(step_b, step_h, dst, fn):
        for slot in range(n_slots):
            for half in range(ppb):
                for queue, cp in enumerate(copies(step_b, step_h, dst, slot, half)):
                    fn(cp, queue)

    @pl.when(step == 0)
    def _():
        for_all_copies(b, h, buf, lambda cp, queue: cp.start(priority=queue))

    @pl.when(step + 1 < pl.num_programs(0) * nh)
    def _():
        wrap = h + 1 == nh
        for_all_copies(jnp.where(wrap, b + 1, b), jnp.where(wrap, 0, h + 1), 1 - buf,
                       lambda cp, queue: cp.start(priority=queue))

    for_all_copies(b, h, buf, lambda cp, queue: cp.wait())

    rows = q_ref.shape[0]
    qb = q_ref[...].astype(BF16)
    n_keys = n_slots * MOBA_BLOCK
    s_past = lax.dot_general(qb, kg_ref[buf].astype(BF16), _NT, preferred_element_type=F32) * scale
    s_new = lax.dot_general(qb, kn_ref[...].astype(BF16), _NT, preferred_element_type=F32) * scale
    r_past = lax.broadcasted_iota(jnp.int32, (rows, n_keys), 0)
    c_past = lax.broadcasted_iota(jnp.int32, (rows, n_keys), 1)
    lo = r_past * (topk * MOBA_BLOCK)
    m_past = jnp.logical_and(c_past >= lo, c_past < lo + topk * MOBA_BLOCK)
    r_new = lax.broadcasted_iota(jnp.int32, (rows, rows), 0)
    c_new = lax.broadcasted_iota(jnp.int32, (rows, rows), 1)
    m_new = jnp.logical_and(c_new <= r_new, c_new < nq)
    s_past = jnp.where(m_past, s_past, NEG_INF)
    s_new = jnp.where(m_new, s_new, NEG_INF)
    top = jnp.maximum(jnp.max(s_past, axis=1, keepdims=True), jnp.max(s_new, axis=1, keepdims=True))
    p_past = jnp.where(m_past, jnp.exp(s_past - top), 0.0)
    p_new = jnp.where(m_new, jnp.exp(s_new - top), 0.0)
    l = jnp.sum(p_past, axis=1, keepdims=True) + jnp.sum(p_new, axis=1, keepdims=True)
    acc = jnp.dot(p_past.astype(BF16), vg_ref[buf].astype(BF16), preferred_element_type=F32)
    acc = acc + jnp.dot(p_new.astype(BF16), vn_ref[...].astype(BF16), preferred_element_type=F32)
    o_ref[...] = (acc / l).astype(o_ref.dtype)


def _moba_sample(q8, k8, v8, cache_k, cache_v, page_table, sel_idx, *, nq, n_heads):
    b, rows, hd = q8.shape
    dh = hd // n_heads
    page = cache_k.shape[1]
    n_keys = nq * MOBA_TOPK * MOBA_BLOCK
    blk3 = pl.BlockSpec((None, rows, dh), lambda bi, h, pt, idx: (bi, 0, h))
    return pl.pallas_call(
        functools.partial(_sattn_body, nq=nq, topk=MOBA_TOPK, page=page, scale=dh ** -0.5),
        grid_spec=pltpu.PrefetchScalarGridSpec(
            num_scalar_prefetch=2,
            grid=(b, n_heads),
            in_specs=[blk3, blk3, blk3,
                      pl.BlockSpec(memory_space=pl.ANY), pl.BlockSpec(memory_space=pl.ANY)],
            out_specs=blk3,
            scratch_shapes=[
                pltpu.VMEM((2, n_keys, dh), F32),
                pltpu.VMEM((2, n_keys, dh), F32),
                pltpu.SemaphoreType.DMA((2, 2)),
            ],
        ),
        out_shape=jax.ShapeDtypeStruct((b, rows, hd), BF16),
        compiler_params=_params("arbitrary", "arbitrary"),
        name="moba_sample",
    )(page_table, sel_idx, q8, k8, v8, cache_k, cache_v)


def _rope_tables(pos, dh):
    half = dh // 2
    inv = ROPE_THETA ** (-2.0 * jnp.arange(half, dtype=F32) / dh)
    ang = pos.astype(F32)[:, None] * inv[None, :]
    cos, sin = jnp.cos(ang), jnp.sin(ang)
    return jnp.concatenate([cos, cos], axis=1), jnp.concatenate([-sin, sin], axis=1)


def _pad_rows(a, rows):
    return jnp.pad(a, ((0, 0), (0, rows - a.shape[1]), (0, 0)))


def _trunk(x, n_seq, seq_len, pos, conv_prev, past, w, *, tm, tf, tn):
    (lng, lnb, w_ffn_gate, w_ffn_up, w_ffn_down, w_conv_in, w_conv, w_conv_out,
     w_k, w_v, w_q, w_o) = w
    depth = w_ffn_gate.shape[0]
    n_a = w_conv_in.shape[0]
    alpha = float((2 * depth) ** 0.25)
    m, d = x.shape
    n_heads = w_k.shape[1] // LANES
    dh = w_k.shape[1] // n_heads
    cos, sin_signed = _rope_tables(pos, dh)
    if past is not None:
        cos, sin_signed = jnp.tile(cos, (n_seq, 1)), jnp.tile(sin_signed, (n_seq, 1))
    wide = 2 * tn if w_k.shape[1] % (2 * tn) == 0 else tn
    wider = 2 * wide if w_k.shape[1] % (2 * wide) == 0 else wide
    conv_states = []
    for layer in range(depth):
        if layer == n_a:
            k_new, v_new = _proj(x, [(w_k, None), (w_v, None)], (True, False), cos, sin_signed,
                                 tm=tm, tn=wide)
            if past is not None:
                means = past[3]()
        x = yield x, layer, 0, 3 * layer
        if layer < n_a:
            if seq_len < tm:
                prev = conv_prev[layer]
                p1 = jnp.zeros((n_seq, seq_len, d), F32).at[:, 0].set(prev[:, 1])
                p2 = jnp.zeros((n_seq, seq_len, d), F32).at[:, 0].set(prev[:, 0]).at[:, 1].set(prev[:, 1])
                prev_arg = (p1.reshape(m, d), p2.reshape(m, d))
            else:
                prev_arg = jnp.pad(conv_prev[layer], ((0, 0), (SUBLANES - (CONV_W - 1), 0), (0, 0)))
            x, tail = _conv(x, w_conv_in, w_conv, w_conv_out, lng, lnb, layer, 3 * layer + 1,
                            prev_arg, alpha=alpha, tm=tm, tn=tn, seq_len=seq_len)
            if seq_len < tm:
                st = tail.reshape(n_seq, seq_len, d)[:, seq_len - (CONV_W - 1):]
            else:
                tps = seq_len // tm
                st = tail[tps - 1::tps, SUBLANES - (CONV_W - 1):]
            conv_states.append(st)
        else:
            jb = layer - n_a
            (q,) = _proj(x, [(w_q, jb)], (True,), cos, sin_signed, tm=tm, tn=wider)
            if past is None:
                a = _moba_prompt(q.reshape(n_seq, seq_len, -1), k_new.reshape(n_seq, seq_len, -1),
                                 v_new.reshape(n_seq, seq_len, -1), n_heads=n_heads,
                                 hps=4 if n_heads % 4 == 0 else 1)
                a = a.reshape(m, -1)
            else:
                q3 = q.reshape(n_seq, seq_len, -1)
                sel = _sample_select(means, q3.reshape(n_seq, seq_len, n_heads, dh), topk=MOBA_TOPK)
                sel = sel[:, :, :MOBA_TOPK, :n_heads].transpose(0, 3, 1, 2).reshape(-1)
                a8 = _moba_sample(_pad_rows(q3, SUBLANES),
                                  _pad_rows(k_new.reshape(n_seq, seq_len, -1), SUBLANES),
                                  _pad_rows(v_new.reshape(n_seq, seq_len, -1), SUBLANES),
                                  past[0], past[1], past[2], sel, nq=seq_len, n_heads=n_heads)
                a = a8[:, :seq_len].reshape(m, -1)
            x = _oproj(x, a, w_o, lng, lnb, jb, 3 * layer + 1, alpha=alpha, tm=tm, tk=wider)
        x = yield x, layer, 1, 3 * layer + 2
    return x, jnp.stack(conv_states), k_new, v_new


def kernel(x_prompt, x_sample, state_conv, cache_k, cache_v, page_table, ln_g, ln_b, w_ffn_gate, w_ffn_up, w_ffn_down, w_conv_in, w_conv, w_conv_out, w_k, w_v, w_q, w_o):
    bp, tp, d = x_prompt.shape
    bs, ts, _ = x_sample.shape
    n_heads, dh = cache_k.shape[2], cache_k.shape[3]
    n_a = w_conv_in.shape[0]
    past_len = page_table.shape[1] * cache_k.shape[1]
    assert dh == LANES and w_conv.shape[1] == CONV_W
    assert tp % MOBA_BLOCK == 0 and past_len % MOBA_BLOCK == 0 and MOBA_BLOCK % cache_k.shape[1] == 0
    assert CONV_W - 1 <= ts <= SUBLANES and past_len // MOBA_BLOCK >= MOBA_TOPK
    lng = ln_g.reshape(-1, 1, d)
    lnb = ln_b.reshape(-1, 1, d)
    w = (lng, lnb, w_ffn_gate, w_ffn_up, w_ffn_down, w_conv_in, w_conv, w_conv_out, w_k, w_v, w_q, w_o)
    f = w_ffn_gate.shape[-1]
    tf = 256 if f % 256 == 0 else f
    tn = 256 if d % 256 == 0 else d

    conv0 = jnp.zeros((n_a, bp, CONV_W - 1, d), F32)
    tm_p = min(1024, tp)
    pos_s = past_len + jnp.arange(ts, dtype=jnp.int32)
    prompt = _trunk(x_prompt.reshape(bp * tp, d), bp, tp, jnp.arange(tp, dtype=jnp.int32),
                    conv0, None, w, tm=tm_p, tf=tf, tn=tn)
    n_blocks = past_len // MOBA_BLOCK
    mean_parts = []

    def block_means():
        if sum(part.shape[0] for part in mean_parts) == bs * n_blocks:
            return jnp.concatenate(mean_parts, axis=0).reshape(bs, n_blocks, n_heads, dh)
        return _cache_means(cache_k, page_table, blocks_per_step=2 if n_blocks % 2 == 0 else 1)

    sample = _trunk(x_sample.reshape(bs * ts, d), bs, ts, pos_s, state_conv,
                    (cache_k, cache_v, page_table, block_means), w, tm=bs * ts, tf=tf, tn=tn)
    alpha = float((2 * w_ffn_gate.shape[0]) ** 0.25)
    req_p, req_s = next(prompt), next(sample)
    results = None
    while results is None:
        (xp, layer, sub, ln_idx), xs = req_p, req_s[0]
        first_block = sum(part.shape[0] for part in mean_parts)
        ride = (cache_k, page_table, first_block) if first_block < bs * n_blocks else None
        yp, ys, *part = _ffn(xp, xs, w_ffn_gate, w_ffn_up, w_ffn_down, lng, lnb, layer, sub, ln_idx,
                             alpha=alpha, tm=tm_p, tf=tf, ride=ride)
        mean_parts += part
        try:
            req_p, req_s = prompt.send(yp), sample.send(ys)
        except StopIteration as done:
            try:
                sample.send(ys)
            except StopIteration as done_s:
                results = done.value, done_s.value
    (y_p, conv_p, k_p, v_p), (y_s, conv_s, k_s, v_s) = results
    return (y_p.reshape(bp, tp, d), y_s.reshape(bs, ts, d), conv_p, conv_s,
            k_p.reshape(bp, tp, n_heads, dh), v_p.reshape(bp, tp, n_heads, dh),
            k_s.reshape(bs, ts, n_heads, dh), v_s.reshape(bs, ts, n_heads, dh))
```
